```python
import math
import jax, jax.numpy as jnp
from jax import lax
import numpy as np

D_MODEL = 2048
BATCH = 4
SEQ = 2048
DEPTH = 2

GRID_W = 64
CTX_LEN = 256
D_MIX = D_MODEL
FOURIER_W = D_MODEL // 4
FOURIER_GROUPS = 4
FOURIER_GW = FOURIER_W // FOURIER_GROUPS
CONV_W = D_MODEL // 4
CONV_K = 31
ATT_W = D_MODEL // 2
DIFF_HEAD_DIM = 64
DIFF_HEADS = ATT_W // (2 * DIFF_HEAD_DIM)
D_FF = 4 * D_MODEL
Q_BLOCK = 128
ROPE_BASE = 10000.0
EPS = 1e-6

OFF_CONV = FOURIER_W
OFF_Q = OFF_CONV + 2 * CONV_W
OFF_K = OFF_Q + ATT_W
OFF_V = OFF_K + ATT_W
IN_COLS = OFF_V + ATT_W

kernel_name = "hybrid_fourier_conformer_diffattn_dit_block"


def rms_norm(x, g):
    xf = x.astype(jnp.float32)
    y = xf * lax.rsqrt(jnp.mean(xf * xf, axis=-1, keepdims=True) + EPS)
    return (y * g.astype(jnp.float32)).astype(x.dtype)


def layer_norm(x, g, b):
    xf = x.astype(jnp.float32)
    mu = jnp.mean(xf, axis=-1, keepdims=True)
    xc = xf - mu
    y = xc * lax.rsqrt(jnp.mean(xc * xc, axis=-1, keepdims=True) + EPS)
    return (y * g.astype(jnp.float32) + b.astype(jnp.float32)).astype(x.dtype)


def modulate(h, shift, scale):
    return h * (1 + scale) + shift


def axial_rope(n):
    rows = n // GRID_W
    row = jnp.repeat(jnp.arange(rows), GRID_W).astype(jnp.float32)
    col = jnp.tile(jnp.arange(GRID_W), rows).astype(jnp.float32)
    n_freq = DIFF_HEAD_DIM // 4
    inv = ROPE_BASE ** (-jnp.arange(n_freq, dtype=jnp.float32) / n_freq)
    ang = jnp.concatenate([row[:, None] * inv, col[:, None] * inv], axis=-1)
    return jnp.cos(ang), jnp.sin(ang)


def apply_rope(t, cos, sin):
    half = DIFF_HEAD_DIM // 2
    tf = t.astype(jnp.float32)
    cs = cos[None, :, None, None, :]
    sn = sin[None, :, None, None, :]
    t1, t2 = tf[..., :half], tf[..., half:]
    out = jnp.concatenate([t1 * cs - t2 * sn, t2 * cs + t1 * sn], axis=-1)
    return out.astype(t.dtype)


def fourier_mixer(u, w_f):
    b, L, _ = u.shape
    ug = u.reshape(b, L, FOURIER_GROUPS, FOURIER_GW).astype(jnp.float32)
    z = jnp.fft.fft2(ug, axes=(1, 3), norm="ortho").real.astype(u.dtype)
    return jnp.einsum('blgc,gce->blge', z, w_f).reshape(b, L, FOURIER_W)


def conv_module(u, w_dw, b_dw, g_ln, b_ln, w_pw, b_pw):
    a, gt = jnp.split(u, 2, axis=-1)
    z = a * jax.nn.sigmoid(gt)
    z = lax.conv_general_dilated(
        z, w_dw[:, None, :], window_strides=(1,),
        padding=[(CONV_K // 2, CONV_K // 2)],
        dimension_numbers=('NWC', 'WIO', 'NWC'),
        feature_group_count=CONV_W) + b_dw
    z = jax.nn.silu(layer_norm(z, g_ln, b_ln))
    return z @ w_pw + b_pw


def diff_attend(q, k, v, lam):
    s = jnp.einsum('bqhmd,bkhmd->bhmqk', q, k).astype(jnp.float32) * (DIFF_HEAD_DIM ** -0.5)
    p = jax.nn.softmax(s, axis=-1)
    a = p[:, :, 0] - lam * p[:, :, 1]
    return jnp.einsum('bhqk,bkhe->bqhe', a.astype(v.dtype), v)


def split_heads_qk(t):
    b, L, _ = t.shape
    return t.reshape(b, L, DIFF_HEADS, 2, DIFF_HEAD_DIM)


def split_heads_v(t):
    b, L, _ = t.shape
    return t.reshape(b, L, DIFF_HEADS, 2 * DIFF_HEAD_DIM)


def setup_inputs(seed: int = 0) -> dict:
    key = jax.random.key(seed)
    ks = jax.random.split(key, 32)
    f32 = jnp.float32
    nrm = lambda k, shape, s: jax.random.normal(k, shape, f32) * s
    gain = lambda k, shape: 1.0 + 0.05 * jax.random.normal(k, shape, f32)
    L = DEPTH
    return {
        "x": nrm(ks[0], (BATCH, SEQ, D_MODEL), 1.0),
        "c": nrm(ks[1], (BATCH, D_MODEL), 1.0),
        "ctx": nrm(ks[2], (BATCH, CTX_LEN, D_MODEL), 1.0),
        "c_ctx": nrm(ks[3], (D_MODEL,), 1.0),
        "w_ada": nrm(ks[4], (L, D_MODEL, 6 * D_MODEL), 0.5 * D_MODEL ** -0.5),
        "b_ada": nrm(ks[5], (L, 6 * D_MODEL), 0.02),
        "g_pre_mix": gain(ks[6], (L, D_MODEL)),
        "g_post_mix": gain(ks[7], (L, D_MODEL)),
        "g_pre_mlp": gain(ks[8], (L, D_MODEL)),
        "g_post_mlp": gain(ks[9], (L, D_MODEL)),
        "w_in": nrm(ks[10], (L, D_MODEL, IN_COLS), D_MODEL ** -0.5),
        "w_out": nrm(ks[11], (L, D_MIX, D_MODEL), D_MIX ** -0.5),
        "w_fourier": nrm(ks[12], (L, FOURIER_GROUPS, FOURIER_GW, FOURIER_GW), FOURIER_GW ** -0.5),
        "w_dw": nrm(ks[13], (L, CONV_K, CONV_W), CONV_K ** -0.5),
        "b_dw": nrm(ks[14], (L, CONV_W), 0.02),
        "g_conv_ln": gain(ks[15], (L, CONV_W)),
        "b_conv_ln": nrm(ks[16], (L, CONV_W), 0.02),
        "w_conv_pw": nrm(ks[17], (L, CONV_W, CONV_W), CONV_W ** -0.5),
        "b_conv_pw": nrm(ks[18], (L, CONV_W), 0.02),
        "lambda_q1": nrm(ks[19], (L, DIFF_HEAD_DIM), 0.1),
        "lambda_k1": nrm(ks[20], (L, DIFF_HEAD_DIM), 0.1),
        "lambda_q2": nrm(ks[21], (L, DIFF_HEAD_DIM), 0.1),
        "lambda_k2": nrm(ks[22], (L, DIFF_HEAD_DIM), 0.1),
        "g_subln": gain(ks[23], (L, 2 * DIFF_HEAD_DIM)),
        "w_mlp_in": nrm(ks[24], (L, D_MODEL, D_FF), D_MODEL ** -0.5),
        "w_mlp_out": nrm(ks[25], (L, D_FF, D_MODEL), D_FF ** -0.5),
    }


def reference(x, c, ctx, c_ctx, w_ada, b_ada, g_pre_mix, g_post_mix, g_pre_mlp, g_post_mlp,
              w_in, w_out, w_fourier, w_dw, b_dw, g_conv_ln, b_conv_ln, w_conv_pw, b_conv_pw,
              lambda_q1, lambda_k1, lambda_q2, lambda_k2, g_subln, w_mlp_in, w_mlp_out):
    bsz, n, _ = x.shape
    n_blocks = n // Q_BLOCK
    cos, sin = axial_rope(n)
    cx = ctx
    for l in range(DEPTH):
        last = l == DEPTH - 1
        mod_lat = (jax.nn.silu(c) @ w_ada[l] + b_ada[l])[:, None, :]
        mod_ctx = jax.nn.silu(c_ctx) @ w_ada[l] + b_ada[l]
        sh1, sc1, g1, sh2, sc2, g2 = jnp.split(mod_lat, 6, axis=-1)
        csh1, csc1, cg1, csh2, csc2, cg2 = jnp.split(mod_ctx, 6, axis=-1)

        lam_init = 0.8 - 0.6 * math.exp(-0.3 * l)
        lq1 = lambda_q1[l].astype(jnp.float32); lk1 = lambda_k1[l].astype(jnp.float32)
        lq2 = lambda_q2[l].astype(jnp.float32); lk2 = lambda_k2[l].astype(jnp.float32)
        lam = jnp.exp(jnp.sum(lq1 * lk1)) - jnp.exp(jnp.sum(lq2 * lk2)) + lam_init

        def attn_post(o):
            o = rms_norm(o, g_subln[l]) * (1 - lam_init)
            return o.reshape(o.shape[0], o.shape[1], ATT_W)

        def mixer_concat(f_in, cv_in, attn_out):
            yf = fourier_mixer(f_in, w_fourier[l])
            yc = conv_module(cv_in, w_dw[l], b_dw[l], g_conv_ln[l], b_conv_ln[l],
                             w_conv_pw[l], b_conv_pw[l])
            return jnp.concatenate([yf, yc, attn_out], axis=-1) @ w_out[l]

        h = modulate(rms_norm(x, g_pre_mix[l]), sh1, sc1)
        hc = modulate(rms_norm(cx, g_pre_mix[l]), csh1, csc1)
        p = h @ w_in[l]
        f_in, cv_in, q, k, v = jnp.split(p, [OFF_CONV, OFF_Q, OFF_K, OFF_V], axis=-1)
        pc_kv = hc @ w_in[l][:, OFF_K:]
        kc, vc = jnp.split(pc_kv, 2, axis=-1)

        q_h = apply_rope(split_heads_qk(q), cos, sin)
        k_h = apply_rope(split_heads_qk(k), cos, sin)
        kc_h = split_heads_qk(kc)
        k_all = jnp.concatenate([kc_h, k_h], axis=1)
        v_all = jnp.concatenate([split_heads_v(vc), split_heads_v(v)], axis=1)

        qb = jnp.moveaxis(q_h.reshape(bsz, n_blocks, Q_BLOCK, DIFF_HEADS, 2, DIFF_HEAD_DIM), 1, 0)
        ob = lax.map(lambda qq: diff_attend(qq, k_all, v_all, lam), qb)
        o = jnp.moveaxis(ob, 0, 1).reshape(bsz, n, DIFF_HEADS, 2 * DIFF_HEAD_DIM)
        y = mixer_concat(f_in, cv_in, attn_post(o))

        if not last:
            pc_rest = hc @ w_in[l][:, :OFF_K]
            fc_in, cvc_in, qc = jnp.split(pc_rest, [OFF_CONV, OFF_Q], axis=-1)
            oc = diff_attend(split_heads_qk(qc), kc_h, split_heads_v(vc), lam)
            yc = mixer_concat(fc_in, cvc_in, attn_post(oc))
            cx = cx + cg1 * rms_norm(yc, g_post_mix[l])
            hc2 = modulate(rms_norm(cx, g_pre_mlp[l]), csh2, csc2)
            yc2 = jnp.square(jax.nn.relu(hc2 @ w_mlp_in[l])) @ w_mlp_out[l]
            cx = cx + cg2 * rms_norm(yc2, g_post_mlp[l])

        x = x + g1 * rms_norm(y, g_post_mix[l])

        h2 = modulate(rms_norm(x, g_pre_mlp[l]), sh2, sc2)
        y2 = jnp.square(jax.nn.relu(h2 @ w_mlp_in[l])) @ w_mlp_out[l]
        x = x + g2 * rms_norm(y2, g_post_mlp[l])
    return x
```

```python
import functools
import math

import numpy as np
import jax
import jax.numpy as jnp
from jax import lax
from jax.experimental import pallas as pl
from jax.experimental.pallas import tpu as pltpu

F32 = jnp.float32
BF16 = jnp.bfloat16

D_MODEL = 2048
GRID_W = 64
FOURIER_W = 512
FOURIER_GROUPS = 4
FOURIER_GW = 128
CONV_W = 512
CONV_K = 31
ATT_W = 1024
DIFF_HEAD_DIM = 64
DIFF_HEADS = 8
D_FF = 4 * D_MODEL
ROPE_BASE = 10000.0
EPS = 1e-6
IN_COLS = 4608

CHUNK = 512
Q_CHUNKS = (3, 4)
K_CHUNKS = (5, 6)
ALL_CHUNKS = tuple(range(IN_COLS // CHUNK))
KV_CHUNKS = (5, 6, 7, 8)

LANES = 128
V7X_VMEM_LIMIT_BYTES = 56 * 1024 * 1024


def _params(*semantics):
    return pltpu.CompilerParams(dimension_semantics=semantics,
                                vmem_limit_bytes=V7X_VMEM_LIMIT_BYTES)


def _rms(y):
    return y * lax.rsqrt(jnp.mean(y * y, axis=-1, keepdims=True) + EPS)


def _adaln_kernel(c_ref, w_ref, b_ref, o_ref):
    s = jax.nn.silu(c_ref[...]).astype(BF16)
    o_ref[...] = jnp.dot(s, w_ref[...].astype(BF16), preferred_element_type=F32) + b_ref[...]


def _adaln(cvec, w_ada, b_ada):
    depth, d, n6 = w_ada.shape
    tn = 1024
    return pl.pallas_call(
        _adaln_kernel,
        grid=(depth, n6 // tn),
        in_specs=[pl.BlockSpec((8, d), lambda l, j: (0, 0)),
                  pl.BlockSpec((None, d, tn), lambda l, j: (l, 0, j)),
                  pl.BlockSpec((None, 1, tn), lambda l, j: (l, 0, j))],
        out_specs=pl.BlockSpec((None, 8, tn), lambda l, j: (l, 0, j)),
        out_shape=jax.ShapeDtypeStruct((depth, 8, n6), F32),
        compiler_params=_params("arbitrary", "arbitrary"),
        name="adaln",
    )(cvec, w_ada, b_ada.reshape(depth, 1, n6))


def _in_proj_kernel(*refs, chunks, rope):
    if rope:
        x_ref, g_ref, sh_ref, sc_ref, w_ref, cos_ref, sin_ref, o_ref = refs
    else:
        x_ref, g_ref, sh_ref, sc_ref, w_ref, o_ref = refs
    h = _rms(x_ref[...]) * g_ref[...]
    hb = (h * (1.0 + sc_ref[...]) + sh_ref[...]).astype(BF16)
    if rope:
        cos = cos_ref[...]
        sin = sin_ref[...]
        lane = lax.broadcasted_iota(jnp.int32, cos.shape, 1)
        first_half = (lane % DIFF_HEAD_DIM) < (DIFF_HEAD_DIM // 2)
    for n, j in enumerate(chunks):
        acc = jnp.dot(hb, w_ref[:, j * CHUNK:(j + 1) * CHUNK], preferred_element_type=F32)
        if j in Q_CHUNKS or j in K_CHUNKS:
            for s in range(CHUNK // LANES):
                t = acc[:, s * LANES:(s + 1) * LANES]
                if rope:
                    swapped = jnp.where(first_half,
                                        pltpu.roll(t, LANES - DIFF_HEAD_DIM // 2, 1),
                                        pltpu.roll(t, DIFF_HEAD_DIM // 2, 1))
                    t = t * cos + swapped * sin
                if j in Q_CHUNKS:
                    t = t * (DIFF_HEAD_DIM ** -0.5)
                o_ref[:, n * CHUNK + s * LANES:n * CHUNK + (s + 1) * LANES] = t.astype(BF16)
        else:
            o_ref[:, n * CHUNK:(n + 1) * CHUNK] = acc.astype(BF16)


def _in_proj(x, g_pre, modv, w, cos, sin, *, chunks, group_rows, tm):
    m, d = x.shape
    bpg = group_rows // tm
    rope = cos is not None
    in_specs = [pl.BlockSpec((tm, d), lambda i: (i, 0)),
                pl.BlockSpec((1, d), lambda i: (0, 0)),
                pl.BlockSpec((None, None, 1, d), lambda i: (i // bpg, 0, 0, 0)),
                pl.BlockSpec((None, None, 1, d), lambda i: (i // bpg, 1, 0, 0)),
                pl.BlockSpec(w.shape, lambda i: (0, 0), pipeline_mode=pl.Buffered(1))]
    args = [x, g_pre.reshape(1, d), modv, modv, w]
    if rope:
        in_specs += [pl.BlockSpec((tm, LANES), lambda i: (i % bpg, 0)),
                     pl.BlockSpec((tm, LANES), lambda i: (i % bpg, 0))]
        args += [cos, sin]
    ncol = len(chunks) * CHUNK
    return pl.pallas_call(
        functools.partial(_in_proj_kernel, chunks=chunks, rope=rope),
        grid=(m // tm,),
        in_specs=in_specs,
        out_specs=pl.BlockSpec((tm, ncol), lambda i: (i, 0)),
        out_shape=jax.ShapeDtypeStruct((m, ncol), BF16),
        compiler_params=_params("arbitrary"),
        name="in_proj",
    )(*args)


def _attn_kernel(*refs, n_kv, lam_init):
    q_ref = refs[0]
    kv_refs = refs[1:1 + 2 * n_kv]
    lamp_ref, gsub_ref, o_ref = refs[1 + 2 * n_kv:]
    tq = q_ref.shape[0]
    q = q_ref[...]
    lane = lax.broadcasted_iota(jnp.int32, q.shape, 1)
    zero = jnp.zeros_like(q)
    qq = jnp.concatenate([jnp.where(lane < DIFF_HEAD_DIM, q, zero),
                          jnp.where(lane < DIFF_HEAD_DIM, zero, q)], axis=0)
    scores = [lax.dot_general(qq, kv_refs[2 * i][...], (((1,), (1,)), ((), ())),
                              preferred_element_type=F32) for i in range(n_kv)]
    m = scores[0].max(axis=-1, keepdims=True)
    for s in scores[1:]:
        m = jnp.maximum(m, s.max(axis=-1, keepdims=True))
    es = [jnp.exp(s - m) for s in scores]
    den = es[0].sum(axis=-1, keepdims=True)
    for e in es[1:]:
        den = den + e.sum(axis=-1, keepdims=True)
    lp = lamp_ref[...]
    lam = (jnp.exp(jnp.sum(lp[0:1] * lp[1:2], axis=-1, keepdims=True))
           - jnp.exp(jnp.sum(lp[2:3] * lp[3:4], axis=-1, keepdims=True)) + lam_init)
    rden = 1.0 / den
    c1 = rden[:tq]
    c2 = rden[tq:] * lam
    o = None
    for i in range(n_kv):
        a = (es[i][:tq] * c1 - es[i][tq:] * c2).astype(BF16)
        part = jnp.dot(a, kv_refs[2 * i + 1][...], preferred_element_type=F32)
        o = part if o is None else o + part
    o_ref[...] = (_rms(o) * gsub_ref[...] * (1.0 - lam_init)).astype(BF16)


def _attention(q_arr, q_col, kvs, lamp, gsub, *, batch, q_len, tq, lam_init):
    nqb = q_len // tq
    in_specs = [pl.BlockSpec((tq, LANES), lambda b, h, i: (b * nqb + i, q_col + h))]
    args = [q_arr]
    for arr, k_col, v_col, kv_len in kvs:
        in_specs.append(pl.BlockSpec((kv_len, LANES), lambda b, h, i, c=k_col: (b, c + h)))
        in_specs.append(pl.BlockSpec((kv_len, LANES), lambda b, h, i, c=v_col: (b, c + h)))
        args += [arr, arr]
    in_specs += [pl.BlockSpec(lamp.shape, lambda b, h, i: (0, 0)),
                 pl.BlockSpec(gsub.shape, lambda b, h, i: (0, 0))]
    args += [lamp, gsub]
    return pl.pallas_call(
        functools.partial(_attn_kernel, n_kv=len(kvs), lam_init=lam_init),
        grid=(batch, DIFF_HEADS, nqb),
        in_specs=in_specs,
        out_specs=pl.BlockSpec((tq, LANES), lambda b, h, i: (b * nqb + i, h)),
        out_shape=jax.ShapeDtypeStruct((batch * q_len, ATT_W), BF16),
        compiler_params=_params("arbitrary", "arbitrary", "arbitrary"),
        name="diff_attn",
    )(*args)


def _dft_constants(seq_len):
    k = np.arange(seq_len, dtype=np.int64)
    ang = 2.0 * np.pi * ((k[:, None] * k[None, :]) % seq_len).astype(np.float64) / seq_len
    cs = np.concatenate([np.cos(ang), -np.sin(ang)], axis=1).astype(np.float32)
    c = np.arange(FOURIER_GW, dtype=np.int64)
    angc = 2.0 * np.pi * ((c[:, None] * c[None, :]) % FOURIER_GW).astype(np.float64) / FOURIER_GW
    norm = 1.0 / math.sqrt(seq_len * FOURIER_GW)
    cc = (np.cos(angc) * norm).astype(np.float32)
    sc = (np.sin(angc) * norm).astype(np.float32)
    return jnp.asarray(cs).astype(BF16), jnp.asarray(cc).astype(BF16), jnp.asarray(sc).astype(BF16)


def _fourier_kernel(u_ref, wf_ref, cc_ref, sc_ref, cs_ref, o_ref, r_scr):
    seq_len = u_ref.shape[0]

    @pl.when(pl.program_id(1) == 0)
    def _():
        for g in range(FOURIER_GROUPS):
            wf = wf_ref[g].astype(BF16)
            a = jnp.dot(cc_ref[...], wf, preferred_element_type=F32).astype(BF16)
            b = jnp.dot(sc_ref[...], wf, preferred_element_type=F32).astype(BF16)
            u = u_ref[:, g * FOURIER_GW:(g + 1) * FOURIER_GW]
            r_scr[0:seq_len, g * FOURIER_GW:(g + 1) * FOURIER_GW] = (
                jnp.dot(u, a, preferred_element_type=F32).astype(BF16))
            r_scr[seq_len:2 * seq_len, g * FOURIER_GW:(g + 1) * FOURIER_GW] = (
                jnp.dot(u, b, preferred_element_type=F32).astype(BF16))

    o_ref[...] = jnp.dot(cs_ref[...], r_scr[...], preferred_element_type=F32).astype(BF16)


def _fourier(p, w_f, *, batch, seq_len, tl):
    cs, cc, sc = _dft_constants(seq_len)
    nlb = seq_len // tl
    return pl.pallas_call(
        _fourier_kernel,
        grid=(batch, nlb),
        in_specs=[pl.BlockSpec((seq_len, FOURIER_W), lambda b, j: (b, 0)),
                  pl.BlockSpec(w_f.shape, lambda b, j: (0, 0, 0)),
                  pl.BlockSpec(cc.shape, lambda b, j: (0, 0)),
                  pl.BlockSpec(sc.shape, lambda b, j: (0, 0)),
                  pl.BlockSpec((tl, 2 * seq_len), lambda b, j: (j, 0))],
        out_specs=pl.BlockSpec((tl, FOURIER_W), lambda b, j: (b * nlb + j, 0)),
        out_shape=jax.ShapeDtypeStruct((batch * seq_len, FOURIER_W), BF16),
        scratch_shapes=[pltpu.VMEM((2 * seq_len, FOURIER_W), BF16)],
        compiler_params=_params("arbitrary", "arbitrary"),
        name="fourier",
    )(p, w_f, cc, sc, cs)


CONV_PAD = 16
CONV_ROWS = 128


def _conv_kernel(a_ref, gt_ref, wdw_ref, bdw_ref, gln_ref, bln_ref, wpw_ref, bpw_ref, o_ref, zp_scr, win_scr):
    seq_len = a_ref.shape[0]
    zp_scr[0:CONV_PAD, :] = jnp.zeros((CONV_PAD, CONV_W), F32)
    zp_scr[CONV_PAD + seq_len:2 * CONV_PAD + seq_len, :] = jnp.zeros((CONV_PAD, CONV_W), F32)
    zp_scr[CONV_PAD:CONV_PAD + seq_len, :] = (a_ref[...].astype(F32)
                                              * jax.nn.sigmoid(gt_ref[...].astype(F32)))

    def body(r, carry):
        base = pl.multiple_of(r * CONV_ROWS, CONV_ROWS)
        win_scr[...] = zp_scr[pl.ds(base, CONV_ROWS + 2 * CONV_PAD), :]
        cols = []
        for c in range(CONV_W // LANES):
            acc = jnp.zeros((CONV_ROWS, LANES), F32)
            for k in range(CONV_K):
                off = CONV_PAD - CONV_K // 2 + k
                acc = acc + (win_scr[off:off + CONV_ROWS, c * LANES:(c + 1) * LANES]
                             * wdw_ref[k:k + 1, c * LANES:(c + 1) * LANES])
            cols.append(acc)
        z = jnp.concatenate(cols, axis=1) + bdw_ref[...]
        zc = z - jnp.mean(z, axis=-1, keepdims=True)
        y = zc * lax.rsqrt(jnp.mean(zc * zc, axis=-1, keepdims=True) + EPS) * gln_ref[...] + bln_ref[...]
        s = (y * jax.nn.sigmoid(y)).astype(BF16)
        out = jnp.dot(s, wpw_ref[...], preferred_element_type=F32) + bpw_ref[...]
        o_ref[pl.ds(base, CONV_ROWS), :] = out.astype(BF16)
        return carry

    lax.fori_loop(0, seq_len // CONV_ROWS, body, 0)


def _conv(p, w_dw, b_dw, g_ln, b_ln, w_pw, b_pw, *, batch, seq_len):
    wdw = jnp.concatenate([w_dw, jnp.zeros((1, CONV_W), F32)], axis=0)
    vec = lambda v: pl.BlockSpec((1, CONV_W), lambda b: (0, 0))
    return pl.pallas_call(
        _conv_kernel,
        grid=(batch,),
        in_specs=[pl.BlockSpec((seq_len, CONV_W), lambda b: (b, 1)),
                  pl.BlockSpec((seq_len, CONV_W), lambda b: (b, 2)),
                  pl.BlockSpec(wdw.shape, lambda b: (0, 0)),
                  vec(b_dw), vec(g_ln), vec(b_ln),
                  pl.BlockSpec(w_pw.shape, lambda b: (0, 0)),
                  vec(b_pw)],
        out_specs=pl.BlockSpec((seq_len, CONV_W), lambda b: (b, 0)),
        out_shape=jax.ShapeDtypeStruct((batch * seq_len, CONV_W), BF16),
        scratch_shapes=[pltpu.VMEM((seq_len + 2 * CONV_PAD, CONV_W), F32),
                        pltpu.VMEM((CONV_ROWS + 2 * CONV_PAD, CONV_W), F32)],
        compiler_params=_params("arbitrary"),
        name="conv_module",
    )(p, p, wdw, b_dw.reshape(1, -1), g_ln.reshape(1, -1), b_ln.reshape(1, -1), w_pw, b_pw.reshape(1, -1))


def _out_proj_kernel(yf_ref, yc_ref, ao_ref, w_ref, x_ref, gate_ref, gpost_ref, o_ref):
    y = jnp.dot(yf_ref[...], w_ref[0:FOURIER_W, :], preferred_element_type=F32)
    y = y + jnp.dot(yc_ref[...], w_ref[FOURIER_W:FOURIER_W + CONV_W, :], preferred_element_type=F32)
    y = y + jnp.dot(ao_ref[...], w_ref[FOURIER_W + CONV_W:, :], preferred_element_type=F32)
    o_ref[...] = x_ref[...] + gate_ref[...] * (_rms(y) * gpost_ref[...])


def _out_proj(yf, yc, ao, w, x, modv, g_post, *, group_rows, tm):
    m, d = x.shape
    bpg = group_rows // tm
    return pl.pallas_call(
        _out_proj_kernel,
        grid=(m // tm,),
        in_specs=[pl.BlockSpec((tm, FOURIER_W), lambda i: (i, 0)),
                  pl.BlockSpec((tm, CONV_W), lambda i: (i, 0)),
                  pl.BlockSpec((tm, ATT_W), lambda i: (i, 0)),
                  pl.BlockSpec(w.shape, lambda i: (0, 0), pipeline_mode=pl.Buffered(1)),
                  pl.BlockSpec((tm, d), lambda i: (i, 0)),
                  pl.BlockSpec((None, None, 1, d), lambda i: (i // bpg, 2, 0, 0)),
                  pl.BlockSpec((1, d), lambda i: (0, 0))],
        out_specs=pl.BlockSpec((tm, d), lambda i: (i, 0)),
        out_shape=jax.ShapeDtypeStruct((m, d), F32),
        compiler_params=_params("arbitrary"),
        name="out_proj",
    )(yf, yc, ao, w, x, modv, g_post.reshape(1, d))


def _mlp_kernel(x_ref, g_ref, sh_ref, sc_ref, gate_ref, gpost_ref, w1_ref, w2_ref, o_ref, h_scr, acc_scr):
    j = pl.program_id(1)

    @pl.when(j == 0)
    def _():
        h = _rms(x_ref[...]) * g_ref[...]
        h_scr[...] = (h * (1.0 + sc_ref[...]) + sh_ref[...]).astype(BF16)
        acc_scr[...] = jnp.zeros_like(acc_scr)

    h1 = jnp.maximum(jnp.dot(h_scr[...], w1_ref[...], preferred_element_type=F32), 0.0)
    acc_scr[...] += jnp.dot((h1 * h1).astype(BF16), w2_ref[...], preferred_element_type=F32)

    @pl.when(j == pl.num_programs(1) - 1)
    def _():
        o_ref[...] = x_ref[...] + gate_ref[...] * (_rms(acc_scr[...]) * gpost_ref[...])


def _mlp(x, g_pre, modv, g_post, w1, w2, *, group_rows, tm, tf):
    m, d = x.shape
    dff = w1.shape[1]
    bpg = group_rows // tm
    mod_spec = lambda k: pl.BlockSpec((None, None, 1, d), lambda i, j: (i // bpg, k, 0, 0))
    return pl.pallas_call(
        _mlp_kernel,
        grid=(m // tm, dff // tf),
        in_specs=[pl.BlockSpec((tm, d), lambda i, j: (i, 0)),
                  pl.BlockSpec((1, d), lambda i, j: (0, 0)),
                  mod_spec(3), mod_spec(4), mod_spec(5),
                  pl.BlockSpec((1, d), lambda i, j: (0, 0)),
                  pl.BlockSpec((d, tf), lambda i, j: (0, j)),
                  pl.BlockSpec((tf, d), lambda i, j: (j, 0))],
        out_specs=pl.BlockSpec((tm, d), lambda i, j: (i, 0)),
        out_shape=jax.ShapeDtypeStruct((m, d), F32),
        scratch_shapes=[pltpu.VMEM((tm, d), BF16), pltpu.VMEM((tm, d), F32)],
        compiler_params=_params("arbitrary", "arbitrary"),
        name="mlp",
    )(x, g_pre.reshape(1, d), modv, modv, modv, g_post.reshape(1, d), w1, w2)


def _rope_tables(n):
    rows = n // GRID_W
    row = jnp.repeat(jnp.arange(rows), GRID_W).astype(F32)
    col = jnp.tile(jnp.arange(GRID_W), rows).astype(F32)
    n_freq = DIFF_HEAD_DIM // 4
    inv = ROPE_BASE ** (-jnp.arange(n_freq, dtype=F32) / n_freq)
    ang = jnp.concatenate([row[:, None] * inv, col[:, None] * inv], axis=-1)
    cos, sin = jnp.cos(ang), jnp.sin(ang)
    return (jnp.concatenate([cos, cos, cos, cos], axis=-1),
            jnp.concatenate([-sin, sin, -sin, sin], axis=-1))


def kernel(x, c, ctx, c_ctx, w_ada, b_ada, g_pre_mix, g_post_mix, g_pre_mlp, g_post_mlp, w_in, w_out, w_fourier, w_dw, b_dw, g_conv_ln, b_conv_ln, w_conv_pw, b_conv_pw, lambda_q1, lambda_k1, lambda_q2, lambda_k2, g_subln, w_mlp_in, w_mlp_out):
    bsz, n, d = x.shape
    n_ctx = ctx.shape[1]
    depth = w_ada.shape[0]

    cvec = jnp.zeros((8, d), F32).at[:bsz].set(c).at[bsz].set(c_ctx)
    mod = _adaln(cvec, w_ada, b_ada)

    w_in_b = w_in.astype(BF16)
    w_out_b = w_out.astype(BF16)
    w_pw_b = w_conv_pw.astype(BF16)
    w1_b = w_mlp_in.astype(BF16)
    w2_b = w_mlp_out.astype(BF16)
    cos, sin = _rope_tables(n)

    xl = x.reshape(bsz * n, d)
    xc = ctx.reshape(bsz * n_ctx, d)
    q_col = Q_CHUNKS[0] * CHUNK // LANES
    k_col = K_CHUNKS[0] * CHUNK // LANES
    v_col = KV_CHUNKS[2] * CHUNK // LANES

    for l in range(depth):
        last = l == depth - 1
        lam_init = 0.8 - 0.6 * math.exp(-0.3 * l)
        mod_lat = mod[l, :bsz].reshape(bsz, 6, 1, d)
        mod_ctx = mod[l, bsz:bsz + 1].reshape(1, 6, 1, d)
        lamp = jnp.zeros((8, LANES), F32)
        lamp = lamp.at[0, :DIFF_HEAD_DIM].set(lambda_q1[l]).at[1, :DIFF_HEAD_DIM].set(lambda_k1[l])
        lamp = lamp.at[2, :DIFF_HEAD_DIM].set(lambda_q2[l]).at[3, :DIFF_HEAD_DIM].set(lambda_k2[l])
        gsub = g_subln[l].reshape(1, LANES)

        p_lat = _in_proj(xl, g_pre_mix[l], mod_lat, w_in_b[l], cos, sin,
                         chunks=ALL_CHUNKS, group_rows=n, tm=512)
        ctx_chunks = KV_CHUNKS if last else ALL_CHUNKS
        p_ctx = _in_proj(xc, g_pre_mix[l], mod_ctx, w_in_b[l], None, None,
                         chunks=ctx_chunks, group_rows=bsz * n_ctx, tm=512)
        kc_col = ctx_chunks.index(K_CHUNKS[0]) * CHUNK // LANES
        vc_col = ctx_chunks.index(KV_CHUNKS[2]) * CHUNK // LANES

        ao = _attention(p_lat, q_col, [(p_ctx, kc_col, vc_col, n_ctx), (p_lat, k_col, v_col, n)],
                        lamp, gsub, batch=bsz, q_len=n, tq=256, lam_init=lam_init)
        yf = _fourier(p_lat, w_fourier[l], batch=bsz, seq_len=n, tl=512)
        yc = _conv(p_lat, w_dw[l], b_dw[l], g_conv_ln[l], b_conv_ln[l], w_pw_b[l], b_conv_pw[l],
                   batch=bsz, seq_len=n)
        xl_mid = _out_proj(yf, yc, ao, w_out_b[l], xl, mod_lat, g_post_mix[l], group_rows=n, tm=512)

        if not last:
            aoc = _attention(p_ctx, q_col, [(p_ctx, kc_col, vc_col, n_ctx)], lamp, gsub,
                             batch=bsz, q_len=n_ctx, tq=256, lam_init=lam_init)
            yfc = _fourier(p_ctx, w_fourier[l], batch=bsz, seq_len=n_ctx, tl=n_ctx)
            ycc = _conv(p_ctx, w_dw[l], b_dw[l], g_conv_ln[l], b_conv_ln[l], w_pw_b[l], b_conv_pw[l],
                        batch=bsz, seq_len=n_ctx)
            xc_mid = _out_proj(yfc, ycc, aoc, w_out_b[l], xc, mod_ctx, g_post_mix[l],
                               group_rows=bsz * n_ctx, tm=512)
            xc = _mlp(xc_mid, g_pre_mlp[l], mod_ctx, g_post_mlp[l], w1_b[l], w2_b[l],
                      group_rows=bsz * n_ctx, tm=512, tf=1024)

        xl = _mlp(xl_mid, g_pre_mlp[l], mod_lat, g_post_mlp[l], w1_b[l], w2_b[l],
                  group_rows=n, tm=512, tf=1024)

    return xl.reshape(bsz, n, d)
```

```python
import functools
import math

import numpy as np
import jax
import jax.numpy as jnp
from jax import lax
from jax.experimental import pallas as pl
from jax.experimental.pallas import tpu as pltpu

F32 = jnp.float32
BF16 = jnp.bfloat16

D_MODEL = 2048
GRID_W = 64
FOURIER_W = 512
FOURIER_GROUPS = 4
FOURIER_GW = 128
CONV_W = 512
CONV_K = 31
ATT_W = 1024
DIFF_HEAD_DIM = 64
DIFF_HEADS = 8
D_FF = 4 * D_MODEL
ROPE_BASE = 10000.0
EPS = 1e-6
IN_COLS = 4608

CHUNK = 512
Q_CHUNKS = (3, 4)
K_CHUNKS = (5, 6)
ALL_CHUNKS = tuple(range(IN_COLS // CHUNK))
KV_CHUNKS = (5, 6, 7, 8)

Q_SCALE = DIFF_HEAD_DIM ** -0.5 * math.log2(math.e)

LANES = 128
SUBLANES = 8
V7X_VMEM_LIMIT_BYTES = 56 * 1024 * 1024


def _params(*semantics):
    return pltpu.CompilerParams(dimension_semantics=semantics,
                                vmem_limit_bytes=V7X_VMEM_LIMIT_BYTES)


def _rms(y):
    return y * lax.rsqrt(jnp.mean(y * y, axis=-1, keepdims=True) + EPS)


def _layer_spec(w, layer, index_map_tail, block_tail=None, **kw):
    block_tail = w.shape[1:] if block_tail is None else block_tail
    return pl.BlockSpec((None,) + tuple(block_tail), lambda *g: (layer,) + tuple(index_map_tail(*g)), **kw)


def _adaln_kernel(c_ref, w_ref, b_ref, o_ref):
    s = jax.nn.silu(c_ref[...]).astype(BF16)
    o_ref[...] = jnp.dot(s, w_ref[...].astype(BF16), preferred_element_type=F32) + b_ref[...]


def _adaln(cvec, w_ada, b_ada):
    depth, d, n6 = w_ada.shape
    tn = 1024
    return pl.pallas_call(
        _adaln_kernel,
        grid=(depth, n6 // tn),
        in_specs=[pl.BlockSpec((8, d), lambda l, j: (0, 0)),
                  pl.BlockSpec((None, d, tn), lambda l, j: (l, 0, j)),
                  pl.BlockSpec((None, 1, tn), lambda l, j: (l, 0, j))],
        out_specs=pl.BlockSpec((None, 8, tn), lambda l, j: (l, 0, j)),
        out_shape=jax.ShapeDtypeStruct((depth, 8, n6), F32),
        compiler_params=_params("arbitrary", "arbitrary"),
        name="adaln",
    )(cvec, w_ada, b_ada.reshape(depth, 1, n6))


def _in_proj_kernel(*refs, chunks, rope):
    if rope:
        x_ref, g_ref, sh_ref, sc_ref, w_ref, cos_ref, sin_ref, o_ref = refs
    else:
        x_ref, g_ref, sh_ref, sc_ref, w_ref, o_ref = refs
    h = _rms(x_ref[...]) * g_ref[...]
    hb = (h * (1.0 + sc_ref[...]) + sh_ref[...]).astype(BF16)
    if rope:
        cos = cos_ref[...]
        sin = sin_ref[...]
        lane = lax.broadcasted_iota(jnp.int32, cos.shape, 1)
        first_half = (lane % DIFF_HEAD_DIM) < (DIFF_HEAD_DIM // 2)
    for n, j in enumerate(chunks):
        acc = jnp.dot(hb, w_ref[:, j * CHUNK:(j + 1) * CHUNK], preferred_element_type=F32)
        if j in Q_CHUNKS or j in K_CHUNKS:
            for s in range(CHUNK // LANES):
                t = acc[:, s * LANES:(s + 1) * LANES]
                if rope:
                    swapped = jnp.where(first_half,
                                        pltpu.roll(t, LANES - DIFF_HEAD_DIM // 2, 1),
                                        pltpu.roll(t, DIFF_HEAD_DIM // 2, 1))
                    t = t * cos + swapped * sin
                if j in Q_CHUNKS:
                    t = t * Q_SCALE
                o_ref[:, n * CHUNK + s * LANES:n * CHUNK + (s + 1) * LANES] = t.astype(BF16)
        else:
            o_ref[:, n * CHUNK:(n + 1) * CHUNK] = acc.astype(BF16)


def _in_proj(x, g_pre, modv, w, layer, cos, sin, *, chunks, group_rows, tm):
    m, d = x.shape
    bpg = group_rows // tm
    rope = cos is not None
    in_specs = [pl.BlockSpec((tm, d), lambda i: (i, 0)),
                pl.BlockSpec((1, d), lambda i: (0, 0)),
                pl.BlockSpec((None, None, 1, d), lambda i: (i // bpg, 0, 0, 0)),
                pl.BlockSpec((None, None, 1, d), lambda i: (i // bpg, 1, 0, 0)),
                _layer_spec(w, layer, lambda i: (0, 0), pipeline_mode=pl.Buffered(1))]
    args = [x, g_pre.reshape(1, d), modv, modv, w]
    if rope:
        in_specs += [pl.BlockSpec((tm, LANES), lambda i: (i % bpg, 0)),
                     pl.BlockSpec((tm, LANES), lambda i: (i % bpg, 0))]
        args += [cos, sin]
    ncol = len(chunks) * CHUNK
    return pl.pallas_call(
        functools.partial(_in_proj_kernel, chunks=chunks, rope=rope),
        grid=(m // tm,),
        in_specs=in_specs,
        out_specs=pl.BlockSpec((tm, ncol), lambda i: (i, 0)),
        out_shape=jax.ShapeDtypeStruct((m, ncol), BF16),
        compiler_params=_params("arbitrary"),
        name="in_proj",
    )(*args)


ATT_UNIT_ROWS = 256


def _attn_kernel(*refs, kv_lens, lam_init):
    n_kv = len(kv_lens)
    q_ref = refs[0]
    kv_refs = refs[1:1 + 2 * n_kv]
    lamp_ref, gsub_ref, o_ref, kt_scr, v_scr = refs[1 + 2 * n_kv:]

    @pl.when(pl.program_id(2) == 0)
    def _():
        off = 0
        for i, n in enumerate(kv_lens):
            kt_scr[:, off:off + n] = kv_refs[2 * i][...].T
            v_scr[off:off + n, 0:LANES] = kv_refs[2 * i + 1][...]
            off += n
        v_scr[:, LANES:2 * LANES] = jnp.ones((off, LANES), BF16)

    lp = lamp_ref[...]
    lam = (jnp.exp(jnp.sum(lp[0:1] * lp[1:2], axis=-1, keepdims=True))
           - jnp.exp(jnp.sum(lp[2:3] * lp[3:4], axis=-1, keepdims=True)) + lam_init)
    rows = ATT_UNIT_ROWS
    for u in range(q_ref.shape[0] // rows):
        q = q_ref[u * rows:(u + 1) * rows, :]
        lane = lax.broadcasted_iota(jnp.int32, q.shape, 1)
        zero = jnp.zeros_like(q)
        qq = jnp.concatenate([jnp.where(lane < DIFF_HEAD_DIM, q, zero),
                              jnp.where(lane < DIFF_HEAD_DIM, zero, q)], axis=0)
        s = jnp.dot(qq, kt_scr[...], preferred_element_type=F32)
        e = jnp.exp2(s - s.max(axis=-1, keepdims=True)).astype(BF16)
        acc1 = jnp.dot(e[:rows], v_scr[...], preferred_element_type=F32)
        acc2 = jnp.dot(e[rows:], v_scr[...], preferred_element_type=F32)
        o = (acc1[:, 0:LANES] / acc1[:, LANES:2 * LANES]
             - lam * (acc2[:, 0:LANES] / acc2[:, LANES:2 * LANES]))
        o_ref[u * rows:(u + 1) * rows, :] = (_rms(o) * gsub_ref[...] * (1.0 - lam_init)).astype(BF16)


def _attention(q_arr, q_col, kvs, lamp, gsub, *, batch, q_len, tq, lam_init):
    nqb = q_len // tq
    in_specs = [pl.BlockSpec((tq, LANES), lambda b, h, i: (b * nqb + i, q_col + h))]
    args = [q_arr]
    for arr, k_col, v_col, kv_len in kvs:
        in_specs.append(pl.BlockSpec((kv_len, LANES), lambda b, h, i, c=k_col: (b, c + h)))
        in_specs.append(pl.BlockSpec((kv_len, LANES), lambda b, h, i, c=v_col: (b, c + h)))
        args += [arr, arr]
    in_specs += [pl.BlockSpec(lamp.shape, lambda b, h, i: (0, 0)),
                 pl.BlockSpec(gsub.shape, lambda b, h, i: (0, 0))]
    args += [lamp, gsub]
    kv_lens = tuple(kv[3] for kv in kvs)
    return pl.pallas_call(
        functools.partial(_attn_kernel, kv_lens=kv_lens, lam_init=lam_init),
        grid=(batch, DIFF_HEADS, nqb),
        in_specs=in_specs,
        out_specs=pl.BlockSpec((tq, LANES), lambda b, h, i: (b * nqb + i, h)),
        out_shape=jax.ShapeDtypeStruct((batch * q_len, ATT_W), BF16),
        scratch_shapes=[pltpu.VMEM((LANES, sum(kv_lens)), BF16),
                        pltpu.VMEM((sum(kv_lens), 2 * LANES), BF16)],
        compiler_params=_params("arbitrary", "arbitrary", "arbitrary"),
        name="diff_attn",
    )(*args)


def _dft_constants(seq_len):
    k = np.arange(seq_len, dtype=np.int64)
    ang = 2.0 * np.pi * ((k[:, None] * k[None, :]) % seq_len).astype(np.float64) / seq_len
    cs = np.concatenate([np.cos(ang), -np.sin(ang)], axis=1).astype(np.float32)
    c = np.arange(FOURIER_GW, dtype=np.int64)
    angc = 2.0 * np.pi * ((c[:, None] * c[None, :]) % FOURIER_GW).astype(np.float64) / FOURIER_GW
    norm = 1.0 / math.sqrt(seq_len * FOURIER_GW)
    cc = (np.cos(angc) * norm).astype(np.float32)
    sc = (np.sin(angc) * norm).astype(np.float32)
    return jnp.asarray(cs).astype(BF16), jnp.asarray(cc).astype(BF16), jnp.asarray(sc).astype(BF16)


def _fourier_kernel(u_ref, wf_ref, cc_ref, sc_ref, cs_ref, o_ref, r_scr):
    seq_len = u_ref.shape[0]

    @pl.when(pl.program_id(1) == 0)
    def _():
        for g in range(FOURIER_GROUPS):
            wf = wf_ref[g].astype(BF16)
            a = jnp.dot(cc_ref[...], wf, preferred_element_type=F32).astype(BF16)
            b = jnp.dot(sc_ref[...], wf, preferred_element_type=F32).astype(BF16)
            u = u_ref[:, g * FOURIER_GW:(g + 1) * FOURIER_GW]
            r_scr[0:seq_len, g * FOURIER_GW:(g + 1) * FOURIER_GW] = (
                jnp.dot(u, a, preferred_element_type=F32).astype(BF16))
            r_scr[seq_len:2 * seq_len, g * FOURIER_GW:(g + 1) * FOURIER_GW] = (
                jnp.dot(u, b, preferred_element_type=F32).astype(BF16))

    o_ref[...] = jnp.dot(cs_ref[...], r_scr[...], preferred_element_type=F32).astype(BF16)


def _fourier(p, w_f, layer, *, batch, seq_len, tl):
    cs, cc, sc = _dft_constants(seq_len)
    nlb = seq_len // tl
    return pl.pallas_call(
        _fourier_kernel,
        grid=(batch, nlb),
        in_specs=[pl.BlockSpec((seq_len, FOURIER_W), lambda b, j: (b, 0)),
                  _layer_spec(w_f, layer, lambda b, j: (0, 0, 0)),
                  pl.BlockSpec(cc.shape, lambda b, j: (0, 0)),
                  pl.BlockSpec(sc.shape, lambda b, j: (0, 0)),
                  pl.BlockSpec((tl, 2 * seq_len), lambda b, j: (j, 0))],
        out_specs=pl.BlockSpec((tl, FOURIER_W), lambda b, j: (b * nlb + j, 0)),
        out_shape=jax.ShapeDtypeStruct((batch * seq_len, FOURIER_W), BF16),
        scratch_shapes=[pltpu.VMEM((2 * seq_len, FOURIER_W), BF16)],
        compiler_params=_params("arbitrary", "arbitrary"),
        name="fourier",
    )(p, w_f, cc, sc, cs)


CONV_PAD = 16
CONV_ROWS = 128
CONV_LEAD = CONV_PAD - CONV_K // 2


def _conv_kernel(a_ref, gt_ref, wdw_ref, bdw_ref, gln_ref, bln_ref, wpw_ref, bpw_ref, o_ref, zp_scr):
    seq_len = a_ref.shape[0]
    zp_scr[0:CONV_PAD, :] = jnp.zeros((CONV_PAD, CONV_W), F32)
    zp_scr[CONV_PAD + seq_len:2 * CONV_PAD + seq_len, :] = jnp.zeros((CONV_PAD, CONV_W), F32)
    zp_scr[CONV_PAD:CONV_PAD + seq_len, :] = (a_ref[...].astype(F32)
                                              * jax.nn.sigmoid(gt_ref[...].astype(F32)))
    span = CONV_ROWS + 2 * CONV_PAD - SUBLANES

    def body(r, carry):
        base = pl.multiple_of(r * CONV_ROWS, CONV_ROWS)
        cols = []
        for c in range(CONV_W // LANES):
            win = zp_scr[pl.ds(base, CONV_ROWS + 2 * CONV_PAD), c * LANES:(c + 1) * LANES]
            acc = None
            for shift in range(SUBLANES):
                shifted = win if shift == 0 else pltpu.roll(win, win.shape[0] - shift, 0)
                for a in range(span // SUBLANES):
                    k = SUBLANES * a + shift - CONV_LEAD
                    if 0 <= k < CONV_K and SUBLANES * a + CONV_ROWS <= span:
                        term = (shifted[SUBLANES * a:SUBLANES * a + CONV_ROWS]
                                * wdw_ref[k:k + 1, c * LANES:(c + 1) * LANES])
                        acc = term if acc is None else acc + term
            cols.append(acc)
        z = jnp.concatenate(cols, axis=1) + bdw_ref[...]
        zc = z - jnp.mean(z, axis=-1, keepdims=True)
        y = zc * lax.rsqrt(jnp.mean(zc * zc, axis=-1, keepdims=True) + EPS) * gln_ref[...] + bln_ref[...]
        s = (y * jax.nn.sigmoid(y)).astype(BF16)
        out = jnp.dot(s, wpw_ref[...], preferred_element_type=F32) + bpw_ref[...]
        o_ref[pl.ds(base, CONV_ROWS), :] = out.astype(BF16)
        return carry

    lax.fori_loop(0, seq_len // CONV_ROWS, body, 0)


def _conv(p, w_dw, b_dw, g_ln, b_ln, w_pw, layer, b_pw, *, batch, seq_len):
    vec = pl.BlockSpec((1, CONV_W), lambda b: (0, 0))
    return pl.pallas_call(
        _conv_kernel,
        grid=(batch,),
        in_specs=[pl.BlockSpec((seq_len, CONV_W), lambda b: (b, 1)),
                  pl.BlockSpec((seq_len, CONV_W), lambda b: (b, 2)),
                  pl.BlockSpec(w_dw.shape, lambda b: (0, 0)),
                  vec, vec, vec,
                  _layer_spec(w_pw, layer, lambda b: (0, 0)),
                  vec],
        out_specs=pl.BlockSpec((seq_len, CONV_W), lambda b: (b, 0)),
        out_shape=jax.ShapeDtypeStruct((batch * seq_len, CONV_W), BF16),
        scratch_shapes=[pltpu.VMEM((seq_len + 2 * CONV_PAD, CONV_W), F32)],
        compiler_params=_params("arbitrary"),
        name="conv_module",
    )(p, p, w_dw, b_dw.reshape(1, -1), g_ln.reshape(1, -1), b_ln.reshape(1, -1), w_pw, b_pw.reshape(1, -1))


def _out_proj_kernel(yf_ref, yc_ref, ao_ref, w_ref, x_ref, gate_ref, gpost_ref, o_ref):
    y = jnp.dot(yf_ref[...], w_ref[0:FOURIER_W, :], preferred_element_type=F32)
    y = y + jnp.dot(yc_ref[...], w_ref[FOURIER_W:FOURIER_W + CONV_W, :], preferred_element_type=F32)
    y = y + jnp.dot(ao_ref[...], w_ref[FOURIER_W + CONV_W:, :], preferred_element_type=F32)
    o_ref[...] = x_ref[...] + gate_ref[...] * (_rms(y) * gpost_ref[...])


def _out_proj(yf, yc, ao, w, layer, x, modv, g_post, *, group_rows, tm):
    m, d = x.shape
    bpg = group_rows // tm
    return pl.pallas_call(
        _out_proj_kernel,
        grid=(m // tm,),
        in_specs=[pl.BlockSpec((tm, FOURIER_W), lambda i: (i, 0)),
                  pl.BlockSpec((tm, CONV_W), lambda i: (i, 0)),
                  pl.BlockSpec((tm, ATT_W), lambda i: (i, 0)),
                  _layer_spec(w, layer, lambda i: (0, 0), pipeline_mode=pl.Buffered(1)),
                  pl.BlockSpec((tm, d), lambda i: (i, 0)),
                  pl.BlockSpec((None, None, 1, d), lambda i: (i // bpg, 2, 0, 0)),
                  pl.BlockSpec((1, d), lambda i: (0, 0))],
        out_specs=pl.BlockSpec((tm, d), lambda i: (i, 0)),
        out_shape=jax.ShapeDtypeStruct((m, d), F32),
        compiler_params=_params("arbitrary"),
        name="out_proj",
    )(yf, yc, ao, w, x, modv, g_post.reshape(1, d))


def _mlp_kernel(x_ref, g_ref, sh_ref, sc_ref, gate_ref, gpost_ref, w1_ref, w2_ref, o_ref, h_scr, acc_scr):
    j = pl.program_id(1)

    @pl.when(j == 0)
    def _():
        h = _rms(x_ref[...]) * g_ref[...]
        h_scr[...] = (h * (1.0 + sc_ref[...]) + sh_ref[...]).astype(BF16)
        acc_scr[...] = jnp.zeros_like(acc_scr)

    h1 = jnp.maximum(jnp.dot(h_scr[...], w1_ref[...], preferred_element_type=F32), 0.0)
    acc_scr[...] += jnp.dot((h1 * h1).astype(BF16), w2_ref[...], preferred_element_type=F32)

    @pl.when(j == pl.num_programs(1) - 1)
    def _():
        o_ref[...] = x_ref[...] + gate_ref[...] * (_rms(acc_scr[...]) * gpost_ref[...])


def _mlp(x, g_pre, modv, g_post, w1, w2, layer, *, group_rows, tm, tf):
    m, d = x.shape
    dff = w1.shape[-1]
    bpg = group_rows // tm
    mod_spec = lambda k: pl.BlockSpec((None, None, 1, d), lambda i, j: (i // bpg, k, 0, 0))
    return pl.pallas_call(
        _mlp_kernel,
        grid=(m // tm, dff // tf),
        in_specs=[pl.BlockSpec((tm, d), lambda i, j: (i, 0)),
                  pl.BlockSpec((1, d), lambda i, j: (0, 0)),
                  mod_spec(3), mod_spec(4), mod_spec(5),
                  pl.BlockSpec((1, d), lambda i, j: (0, 0)),
                  _layer_spec(w1, layer, lambda i, j: (0, j), block_tail=(d, tf)),
                  _layer_spec(w2, layer, lambda i, j: (j, 0), block_tail=(tf, d))],
        out_specs=pl.BlockSpec((tm, d), lambda i, j: (i, 0)),
        out_shape=jax.ShapeDtypeStruct((m, d), F32),
        scratch_shapes=[pltpu.VMEM((tm, d), BF16), pltpu.VMEM((tm, d), F32)],
        compiler_params=_params("arbitrary", "arbitrary"),
        name="mlp",
    )(x, g_pre.reshape(1, d), modv, modv, modv, g_post.reshape(1, d), w1, w2)


def _rope_tables(n):
    rows = n // GRID_W
    row = jnp.repeat(jnp.arange(rows), GRID_W).astype(F32)
    col = jnp.tile(jnp.arange(GRID_W), rows).astype(F32)
    n_freq = DIFF_HEAD_DIM // 4
    inv = ROPE_BASE ** (-jnp.arange(n_freq, dtype=F32) / n_freq)
    ang = jnp.concatenate([row[:, None] * inv, col[:, None] * inv], axis=-1)
    cos, sin = jnp.cos(ang), jnp.sin(ang)
    return (jnp.concatenate([cos, cos, cos, cos], axis=-1),
            jnp.concatenate([-sin, sin, -sin, sin], axis=-1))


def kernel(x, c, ctx, c_ctx, w_ada, b_ada, g_pre_mix, g_post_mix, g_pre_mlp, g_post_mlp, w_in, w_out, w_fourier, w_dw, b_dw, g_conv_ln, b_conv_ln, w_conv_pw, b_conv_pw, lambda_q1, lambda_k1, lambda_q2, lambda_k2, g_subln, w_mlp_in, w_mlp_out):
    bsz, n, d = x.shape
    n_ctx = ctx.shape[1]
    depth = w_ada.shape[0]

    cvec = jnp.zeros((8, d), F32).at[:bsz].set(c).at[bsz].set(c_ctx)
    mod = _adaln(cvec, w_ada, b_ada)

    w_in_b = w_in.astype(BF16)
    w_out_b = w_out.astype(BF16)
    w_pw_b = w_conv_pw.astype(BF16)
    w1_b = w_mlp_in.astype(BF16)
    w2_b = w_mlp_out.astype(BF16)
    cos, sin = _rope_tables(n)

    xl = x.reshape(bsz * n, d)
    xc = ctx.reshape(bsz * n_ctx, d)
    q_col = Q_CHUNKS[0] * CHUNK // LANES
    k_col = K_CHUNKS[0] * CHUNK // LANES
    v_col = KV_CHUNKS[2] * CHUNK // LANES

    for l in range(depth):
        last = l == depth - 1
        lam_init = 0.8 - 0.6 * math.exp(-0.3 * l)
        mod_lat = mod[l, :bsz].reshape(bsz, 6, 1, d)
        mod_ctx = mod[l, bsz:bsz + 1].reshape(1, 6, 1, d)
        lamp = jnp.zeros((8, LANES), F32)
        lamp = lamp.at[0, :DIFF_HEAD_DIM].set(lambda_q1[l]).at[1, :DIFF_HEAD_DIM].set(lambda_k1[l])
        lamp = lamp.at[2, :DIFF_HEAD_DIM].set(lambda_q2[l]).at[3, :DIFF_HEAD_DIM].set(lambda_k2[l])
        gsub = g_subln[l].reshape(1, LANES)

        p_lat = _in_proj(xl, g_pre_mix[l], mod_lat, w_in_b, l, cos, sin,
                         chunks=ALL_CHUNKS, group_rows=n, tm=512)
        ctx_chunks = KV_CHUNKS if last else ALL_CHUNKS
        p_ctx = _in_proj(xc, g_pre_mix[l], mod_ctx, w_in_b, l, None, None,
                         chunks=ctx_chunks, group_rows=bsz * n_ctx, tm=512)
        kc_col = ctx_chunks.index(K_CHUNKS[0]) * CHUNK // LANES
        vc_col = ctx_chunks.index(KV_CHUNKS[2]) * CHUNK // LANES

        ao = _attention(p_lat, q_col, [(p_ctx, kc_col, vc_col, n_ctx), (p_lat, k_col, v_col, n)],
                        lamp, gsub, batch=bsz, q_len=n, tq=1024, lam_init=lam_init)
        yf = _fourier(p_lat, w_fourier, l, batch=bsz, seq_len=n, tl=512)
        yc = _conv(p_lat, w_dw[l], b_dw[l], g_conv_ln[l], b_conv_ln[l], w_pw_b, l, b_conv_pw[l],
                   batch=bsz, seq_len=n)
        xl_mid = _out_proj(yf, yc, ao, w_out_b, l, xl, mod_lat, g_post_mix[l], group_rows=n, tm=512)

        if not last:
            aoc = _attention(p_ctx, q_col, [(p_ctx, kc_col, vc_col, n_ctx)], lamp, gsub,
                             batch=bsz, q_len=n_ctx, tq=n_ctx, lam_init=lam_init)
            yfc = _fourier(p_ctx, w_fourier, l, batch=bsz, seq_len=n_ctx, tl=n_ctx)
            ycc = _conv(p_ctx, w_dw[l], b_dw[l], g_conv_ln[l], b_conv_ln[l], w_pw_b, l, b_conv_pw[l],
                        batch=bsz, seq_len=n_ctx)
            xc_mid = _out_proj(yfc, ycc, aoc, w_out_b, l, xc, mod_ctx, g_post_mix[l],
                               group_rows=bsz * n_ctx, tm=512)
            xc = _mlp(xc_mid, g_pre_mlp[l], mod_ctx, g_post_mlp[l], w1_b, w2_b, l,
                      group_rows=bsz * n_ctx, tm=512, tf=1024)

        xl = _mlp(xl_mid, g_pre_mlp[l], mod_lat, g_post_mlp[l], w1_b, w2_b, l,
                  group_rows=n, tm=512, tf=1024)

    return xl.reshape(bsz, n, d)
```

```python
import functools
import math

import numpy as np
import jax
import jax.numpy as jnp
from jax import lax
from jax.experimental import pallas as pl
from jax.experimental.pallas import tpu as pltpu

F32 = jnp.float32
BF16 = jnp.bfloat16

D_MODEL = 2048
GRID_W = 64
FOURIER_W = 512
FOURIER_GROUPS = 4
FOURIER_GW = 128
CONV_W = 512
CONV_K = 31
ATT_W = 1024
DIFF_HEAD_DIM = 64
DIFF_HEADS = 8
D_FF = 4 * D_MODEL
ROPE_BASE = 10000.0
EPS = 1e-6
IN_COLS = 4608

CHUNK = 512
Q_CHUNKS = (3, 4)
K_CHUNKS = (5, 6)
ALL_CHUNKS = tuple(range(IN_COLS // CHUNK))
KV_CHUNKS = (5, 6, 7, 8)

Q_SCALE = DIFF_HEAD_DIM ** -0.5 * math.log2(math.e)

LANES = 128
SUBLANES = 8
V7X_VMEM_LIMIT_BYTES = 56 * 1024 * 1024


def _params(*semantics):
    return pltpu.CompilerParams(dimension_semantics=semantics,
                                vmem_limit_bytes=V7X_VMEM_LIMIT_BYTES)


def _rms(y):
    return y * lax.rsqrt(jnp.mean(y * y, axis=-1, keepdims=True) + EPS)


def _layer_spec(w, layer, index_map_tail, block_tail=None, **kw):
    block_tail = w.shape[1:] if block_tail is None else block_tail
    return pl.BlockSpec((None,) + tuple(block_tail), lambda *g: (layer,) + tuple(index_map_tail(*g)), **kw)


def _adaln_kernel(c_ref, w_ref, b_ref, o_ref):
    s = jax.nn.silu(c_ref[...]).astype(BF16)
    o_ref[...] = jnp.dot(s, w_ref[...].astype(BF16), preferred_element_type=F32) + b_ref[...]


def _adaln(cvec, w_ada, b_ada):
    depth, d, n6 = w_ada.shape
    tn = 1024
    return pl.pallas_call(
        _adaln_kernel,
        grid=(depth, n6 // tn),
        in_specs=[pl.BlockSpec((8, d), lambda l, j: (0, 0)),
                  pl.BlockSpec((None, d, tn), lambda l, j: (l, 0, j)),
                  pl.BlockSpec((None, 1, tn), lambda l, j: (l, 0, j))],
        out_specs=pl.BlockSpec((None, 8, tn), lambda l, j: (l, 0, j)),
        out_shape=jax.ShapeDtypeStruct((depth, 8, n6), F32),
        compiler_params=_params("arbitrary", "arbitrary"),
        name="adaln",
    )(cvec, w_ada, b_ada.reshape(depth, 1, n6))


def _in_proj_kernel(*refs, chunks, rope):
    if rope:
        x_ref, g_ref, sh_ref, sc_ref, w_ref, cos_ref, sin_ref, o_ref = refs
    else:
        x_ref, g_ref, sh_ref, sc_ref, w_ref, o_ref = refs
    h = _rms(x_ref[...]) * g_ref[...]
    hb = (h * (1.0 + sc_ref[...]) + sh_ref[...]).astype(BF16)
    if rope:
        cos = cos_ref[...]
        sin = sin_ref[...]
        lane = lax.broadcasted_iota(jnp.int32, cos.shape, 1)
        first_half = (lane % DIFF_HEAD_DIM) < (DIFF_HEAD_DIM // 2)
    for n, j in enumerate(chunks):
        acc = jnp.dot(hb, w_ref[:, j * CHUNK:(j + 1) * CHUNK], preferred_element_type=F32)
        if j in Q_CHUNKS or j in K_CHUNKS:
            for s in range(CHUNK // LANES):
                t = acc[:, s * LANES:(s + 1) * LANES]
                if rope:
                    swapped = jnp.where(first_half,
                                        pltpu.roll(t, LANES - DIFF_HEAD_DIM // 2, 1),
                                        pltpu.roll(t, DIFF_HEAD_DIM // 2, 1))
                    t = t * cos + swapped * sin
                if j in Q_CHUNKS:
                    t = t * Q_SCALE
                o_ref[:, n * CHUNK + s * LANES:n * CHUNK + (s + 1) * LANES] = t.astype(BF16)
        else:
            o_ref[:, n * CHUNK:(n + 1) * CHUNK] = acc.astype(BF16)


def _in_proj(x, g_pre, modv, w, layer, cos, sin, *, chunks, group_rows, tm):
    m, d = x.shape
    bpg = group_rows // tm
    rope = cos is not None
    in_specs = [pl.BlockSpec((tm, d), lambda i: (i, 0)),
                pl.BlockSpec((1, d), lambda i: (0, 0)),
                pl.BlockSpec((None, None, 1, d), lambda i: (i // bpg, 0, 0, 0)),
                pl.BlockSpec((None, None, 1, d), lambda i: (i // bpg, 1, 0, 0)),
                _layer_spec(w, layer, lambda i: (0, 0), pipeline_mode=pl.Buffered(1))]
    args = [x, g_pre.reshape(1, d), modv, modv, w]
    if rope:
        in_specs += [pl.BlockSpec((tm, LANES), lambda i: (i % bpg, 0)),
                     pl.BlockSpec((tm, LANES), lambda i: (i % bpg, 0))]
        args += [cos, sin]
    ncol = len(chunks) * CHUNK
    return pl.pallas_call(
        functools.partial(_in_proj_kernel, chunks=chunks, rope=rope),
        grid=(m // tm,),
        in_specs=in_specs,
        out_specs=pl.BlockSpec((tm, ncol), lambda i: (i, 0)),
        out_shape=jax.ShapeDtypeStruct((m, ncol), BF16),
        compiler_params=_params("arbitrary"),
        name="in_proj",
    )(*args)


ATT_UNIT_ROWS = 256


def _attn_kernel(*refs, kv_lens, lam_init):
    n_kv = len(kv_lens)
    q_ref = refs[0]
    kv_refs = refs[1:1 + 2 * n_kv]
    lamp_ref, gsub_ref, o_ref, kt_scr, v_scr = refs[1 + 2 * n_kv:]

    @pl.when(pl.program_id(2) == 0)
    def _():
        off = 0
        for i, n in enumerate(kv_lens):
            kt_scr[:, off:off + n] = kv_refs[2 * i][...].T
            v_scr[off:off + n, 0:LANES] = kv_refs[2 * i + 1][...]
            off += n
        v_scr[:, LANES:2 * LANES] = jnp.ones((off, LANES), BF16)

    lp = lamp_ref[...]
    lam = (jnp.exp(jnp.sum(lp[0:1] * lp[1:2], axis=-1, keepdims=True))
           - jnp.exp(jnp.sum(lp[2:3] * lp[3:4], axis=-1, keepdims=True)) + lam_init)
    rows = ATT_UNIT_ROWS
    n_units = q_ref.shape[0] // rows

    def scores(u):
        q = q_ref[u * rows:(u + 1) * rows, :]
        lane = lax.broadcasted_iota(jnp.int32, q.shape, 1)
        zero = jnp.zeros_like(q)
        qq = jnp.concatenate([jnp.where(lane < DIFF_HEAD_DIM, q, zero),
                              jnp.where(lane < DIFF_HEAD_DIM, zero, q)], axis=0)
        return jnp.dot(qq, kt_scr[...], preferred_element_type=F32)

    s_next = scores(0)
    for u in range(n_units):
        s = s_next
        if u + 1 < n_units:
            s_next = scores(u + 1)
        e = jnp.exp2(s - s.max(axis=-1, keepdims=True)).astype(BF16)
        acc1 = jnp.dot(e[:rows], v_scr[...], preferred_element_type=F32)
        acc2 = jnp.dot(e[rows:], v_scr[...], preferred_element_type=F32)
        o = (acc1[:, 0:LANES] / acc1[:, LANES:2 * LANES]
             - lam * (acc2[:, 0:LANES] / acc2[:, LANES:2 * LANES]))
        o_ref[u * rows:(u + 1) * rows, :] = (_rms(o) * gsub_ref[...] * (1.0 - lam_init)).astype(BF16)


def _attention(q_arr, q_col, kvs, lamp, gsub, *, batch, q_len, tq, lam_init):
    nqb = q_len // tq
    in_specs = [pl.BlockSpec((tq, LANES), lambda b, h, i: (b * nqb + i, q_col + h))]
    args = [q_arr]
    for arr, k_col, v_col, kv_len in kvs:
        in_specs.append(pl.BlockSpec((kv_len, LANES), lambda b, h, i, c=k_col: (b, c + h)))
        in_specs.append(pl.BlockSpec((kv_len, LANES), lambda b, h, i, c=v_col: (b, c + h)))
        args += [arr, arr]
    in_specs += [pl.BlockSpec(lamp.shape, lambda b, h, i: (0, 0)),
                 pl.BlockSpec(gsub.shape, lambda b, h, i: (0, 0))]
    args += [lamp, gsub]
    kv_lens = tuple(kv[3] for kv in kvs)
    return pl.pallas_call(
        functools.partial(_attn_kernel, kv_lens=kv_lens, lam_init=lam_init),
        grid=(batch, DIFF_HEADS, nqb),
        in_specs=in_specs,
        out_specs=pl.BlockSpec((tq, LANES), lambda b, h, i: (b * nqb + i, h)),
        out_shape=jax.ShapeDtypeStruct((batch * q_len, ATT_W), BF16),
        scratch_shapes=[pltpu.VMEM((LANES, sum(kv_lens)), BF16),
                        pltpu.VMEM((sum(kv_lens), 2 * LANES), BF16)],
        compiler_params=_params("arbitrary", "arbitrary", "arbitrary"),
        name="diff_attn",
    )(*args)


def _dft_constants(seq_len):
    k = np.arange(seq_len, dtype=np.int64)
    ang = 2.0 * np.pi * ((k[:, None] * k[None, :]) % seq_len).astype(np.float64) / seq_len
    cs = np.concatenate([np.cos(ang), -np.sin(ang)], axis=1).astype(np.float32)
    c = np.arange(FOURIER_GW, dtype=np.int64)
    angc = 2.0 * np.pi * ((c[:, None] * c[None, :]) % FOURIER_GW).astype(np.float64) / FOURIER_GW
    norm = 1.0 / math.sqrt(seq_len * FOURIER_GW)
    cc = (np.cos(angc) * norm).astype(np.float32)
    sc = (np.sin(angc) * norm).astype(np.float32)
    return jnp.asarray(cs).astype(BF16), jnp.asarray(cc).astype(BF16), jnp.asarray(sc).astype(BF16)


def _fourier_kernel(u_ref, wf_ref, cc_ref, sc_ref, cs_ref, o_ref, r_scr):
    seq_len = u_ref.shape[0]

    @pl.when(pl.program_id(1) == 0)
    def _():
        for g in range(FOURIER_GROUPS):
            wf = wf_ref[g].astype(BF16)
            a = jnp.dot(cc_ref[...], wf, preferred_element_type=F32).astype(BF16)
            b = jnp.dot(sc_ref[...], wf, preferred_element_type=F32).astype(BF16)
            u = u_ref[:, g * FOURIER_GW:(g + 1) * FOURIER_GW]
            r_scr[0:seq_len, g * FOURIER_GW:(g + 1) * FOURIER_GW] = (
                jnp.dot(u, a, preferred_element_type=F32).astype(BF16))
            r_scr[seq_len:2 * seq_len, g * FOURIER_GW:(g + 1) * FOURIER_GW] = (
                jnp.dot(u, b, preferred_element_type=F32).astype(BF16))

    o_ref[...] = jnp.dot(cs_ref[...], r_scr[...], preferred_element_type=F32).astype(BF16)


def _fourier(p, w_f, layer, *, batch, seq_len, tl):
    cs, cc, sc = _dft_constants(seq_len)
    nlb = seq_len // tl
    return pl.pallas_call(
        _fourier_kernel,
        grid=(batch, nlb),
        in_specs=[pl.BlockSpec((seq_len, FOURIER_W), lambda b, j: (b, 0)),
                  _layer_spec(w_f, layer, lambda b, j: (0, 0, 0)),
                  pl.BlockSpec(cc.shape, lambda b, j: (0, 0)),
                  pl.BlockSpec(sc.shape, lambda b, j: (0, 0)),
                  pl.BlockSpec((tl, 2 * seq_len), lambda b, j: (j, 0))],
        out_specs=pl.BlockSpec((tl, FOURIER_W), lambda b, j: (b * nlb + j, 0)),
        out_shape=jax.ShapeDtypeStruct((batch * seq_len, FOURIER_W), BF16),
        scratch_shapes=[pltpu.VMEM((2 * seq_len, FOURIER_W), BF16)],
        compiler_params=_params("arbitrary", "arbitrary"),
        name="fourier",
    )(p, w_f, cc, sc, cs)


CONV_PAD = 16
CONV_ROWS = 128
CONV_LEAD = CONV_PAD - CONV_K // 2


def _conv_kernel(a_ref, gt_ref, wdw_ref, bdw_ref, gln_ref, bln_ref, wpw_ref, bpw_ref, o_ref, zp_scr):
    seq_len = a_ref.shape[0]
    zp_scr[0:CONV_PAD, :] = jnp.zeros((CONV_PAD, CONV_W), F32)
    zp_scr[CONV_PAD + seq_len:2 * CONV_PAD + seq_len, :] = jnp.zeros((CONV_PAD, CONV_W), F32)
    zp_scr[CONV_PAD:CONV_PAD + seq_len, :] = (a_ref[...].astype(F32)
                                              * jax.nn.sigmoid(gt_ref[...].astype(F32)))
    span = CONV_ROWS + 2 * CONV_PAD - SUBLANES

    def body(r, carry):
        base = pl.multiple_of(r * CONV_ROWS, CONV_ROWS)
        cols = []
        for c in range(CONV_W // LANES):
            win = zp_scr[pl.ds(base, CONV_ROWS + 2 * CONV_PAD), c * LANES:(c + 1) * LANES]
            acc = None
            for shift in range(SUBLANES):
                shifted = win if shift == 0 else pltpu.roll(win, win.shape[0] - shift, 0)
                for a in range(span // SUBLANES):
                    k = SUBLANES * a + shift - CONV_LEAD
                    if 0 <= k < CONV_K and SUBLANES * a + CONV_ROWS <= span:
                        term = (shifted[SUBLANES * a:SUBLANES * a + CONV_ROWS]
                                * wdw_ref[k:k + 1, c * LANES:(c + 1) * LANES])
                        acc = term if acc is None else acc + term
            cols.append(acc)
        z = jnp.concatenate(cols, axis=1) + bdw_ref[...]
        zc = z - jnp.mean(z, axis=-1, keepdims=True)
        y = zc * lax.rsqrt(jnp.mean(zc * zc, axis=-1, keepdims=True) + EPS) * gln_ref[...] + bln_ref[...]
        s = (y * jax.nn.sigmoid(y)).astype(BF16)
        out = jnp.dot(s, wpw_ref[...], preferred_element_type=F32) + bpw_ref[...]
        o_ref[pl.ds(base, CONV_ROWS), :] = out.astype(BF16)
        return carry

    lax.fori_loop(0, seq_len // CONV_ROWS, body, 0)


def _conv(p, w_dw, b_dw, g_ln, b_ln, w_pw, layer, b_pw, *, batch, seq_len):
    vec = pl.BlockSpec((1, CONV_W), lambda b: (0, 0))
    return pl.pallas_call(
        _conv_kernel,
        grid=(batch,),
        in_specs=[pl.BlockSpec((seq_len, CONV_W), lambda b: (b, 1)),
                  pl.BlockSpec((seq_len, CONV_W), lambda b: (b, 2)),
                  pl.BlockSpec(w_dw.shape, lambda b: (0, 0)),
                  vec, vec, vec,
                  _layer_spec(w_pw, layer, lambda b: (0, 0)),
                  vec],
        out_specs=pl.BlockSpec((seq_len, CONV_W), lambda b: (b, 0)),
        out_shape=jax.ShapeDtypeStruct((batch * seq_len, CONV_W), BF16),
        scratch_shapes=[pltpu.VMEM((seq_len + 2 * CONV_PAD, CONV_W), F32)],
        compiler_params=_params("arbitrary"),
        name="conv_module",
    )(p, p, w_dw, b_dw.reshape(1, -1), g_ln.reshape(1, -1), b_ln.reshape(1, -1), w_pw, b_pw.reshape(1, -1))


def _out_proj_kernel(yf_ref, yc_ref, ao_ref, w_ref, x_ref, gate_ref, gpost_ref, o_ref):
    y = jnp.dot(yf_ref[...], w_ref[0:FOURIER_W, :], preferred_element_type=F32)
    y = y + jnp.dot(yc_ref[...], w_ref[FOURIER_W:FOURIER_W + CONV_W, :], preferred_element_type=F32)
    y = y + jnp.dot(ao_ref[...], w_ref[FOURIER_W + CONV_W:, :], preferred_element_type=F32)
    o_ref[...] = x_ref[...] + gate_ref[...] * (_rms(y) * gpost_ref[...])


def _out_proj(yf, yc, ao, w, layer, x, modv, g_post, *, group_rows, tm):
    m, d = x.shape
    bpg = group_rows // tm
    return pl.pallas_call(
        _out_proj_kernel,
        grid=(m // tm,),
        in_specs=[pl.BlockSpec((tm, FOURIER_W), lambda i: (i, 0)),
                  pl.BlockSpec((tm, CONV_W), lambda i: (i, 0)),
                  pl.BlockSpec((tm, ATT_W), lambda i: (i, 0)),
                  _layer_spec(w, layer, lambda i: (0, 0), pipeline_mode=pl.Buffered(1)),
                  pl.BlockSpec((tm, d), lambda i: (i, 0)),
                  pl.BlockSpec((None, None, 1, d), lambda i: (i // bpg, 2, 0, 0)),
                  pl.BlockSpec((1, d), lambda i: (0, 0))],
        out_specs=pl.BlockSpec((tm, d), lambda i: (i, 0)),
        out_shape=jax.ShapeDtypeStruct((m, d), F32),
        compiler_params=_params("arbitrary"),
        name="out_proj",
    )(yf, yc, ao, w, x, modv, g_post.reshape(1, d))


def _mlp_kernel(x_ref, g_ref, sh_ref, sc_ref, gate_ref, gpost_ref, w1_ref, w2_ref, o_ref, h_scr):
    j = pl.program_id(1)
    last = pl.num_programs(1) - 1

    def ff_chunk(hb):
        h1 = jnp.maximum(jnp.dot(hb, w1_ref[...].astype(BF16), preferred_element_type=F32), 0.0)
        return jnp.dot((h1 * h1).astype(BF16), w2_ref[...].astype(BF16), preferred_element_type=F32)

    @pl.when(j == 0)
    def _():
        h = _rms(x_ref[...]) * g_ref[...]
        hb = (h * (1.0 + sc_ref[...]) + sh_ref[...]).astype(BF16)
        h_scr[...] = hb
        o_ref[...] = ff_chunk(hb)

    @pl.when(jnp.logical_and(j > 0, j < last))
    def _():
        o_ref[...] += ff_chunk(h_scr[...])

    @pl.when(j == last)
    def _():
        y = o_ref[...] + ff_chunk(h_scr[...])
        o_ref[...] = x_ref[...] + gate_ref[...] * (_rms(y) * gpost_ref[...])


def _mlp(x, g_pre, modv, g_post, w1, w2, layer, *, group_rows, tm, tf):
    m, d = x.shape
    dff = w1.shape[-1]
    bpg = group_rows // tm
    mod_spec = lambda k: pl.BlockSpec((None, None, 1, d), lambda i, j: (i // bpg, k, 0, 0))
    return pl.pallas_call(
        _mlp_kernel,
        grid=(m // tm, dff // tf),
        in_specs=[pl.BlockSpec((tm, d), lambda i, j: (i, 0), pipeline_mode=pl.Buffered(1)),
                  pl.BlockSpec((1, d), lambda i, j: (0, 0)),
                  mod_spec(3), mod_spec(4), mod_spec(5),
                  pl.BlockSpec((1, d), lambda i, j: (0, 0)),
                  _layer_spec(w1, layer, lambda i, j: (0, j), block_tail=(d, tf)),
                  _layer_spec(w2, layer, lambda i, j: (j, 0), block_tail=(tf, d))],
        out_specs=pl.BlockSpec((tm, d), lambda i, j: (i, 0)),
        out_shape=jax.ShapeDtypeStruct((m, d), F32),
        scratch_shapes=[pltpu.VMEM((tm, d), BF16)],
        compiler_params=_params("arbitrary", "arbitrary"),
        name="mlp",
    )(x, g_pre.reshape(1, d), modv, modv, modv, g_post.reshape(1, d), w1, w2)


def _rope_tables(n):
    rows = n // GRID_W
    row = jnp.repeat(jnp.arange(rows), GRID_W).astype(F32)
    col = jnp.tile(jnp.arange(GRID_W), rows).astype(F32)
    n_freq = DIFF_HEAD_DIM // 4
    inv = ROPE_BASE ** (-jnp.arange(n_freq, dtype=F32) / n_freq)
    ang = jnp.concatenate([row[:, None] * inv, col[:, None] * inv], axis=-1)
    cos, sin = jnp.cos(ang), jnp.sin(ang)
    return (jnp.concatenate([cos, cos, cos, cos], axis=-1),
            jnp.concatenate([-sin, sin, -sin, sin], axis=-1))


def kernel(x, c, ctx, c_ctx, w_ada, b_ada, g_pre_mix, g_post_mix, g_pre_mlp, g_post_mlp, w_in, w_out, w_fourier, w_dw, b_dw, g_conv_ln, b_conv_ln, w_conv_pw, b_conv_pw, lambda_q1, lambda_k1, lambda_q2, lambda_k2, g_subln, w_mlp_in, w_mlp_out):
    bsz, n, d = x.shape
    n_ctx = ctx.shape[1]
    depth = w_ada.shape[0]

    cvec = jnp.zeros((8, d), F32).at[:bsz].set(c).at[bsz].set(c_ctx)
    mod = _adaln(cvec, w_ada, b_ada)

    w_in_b = w_in.astype(BF16)
    w_out_b = w_out.astype(BF16)
    w_pw_b = w_conv_pw.astype(BF16)
    cos, sin = _rope_tables(n)

    xl = x.reshape(bsz * n, d)
    xc = ctx.reshape(bsz * n_ctx, d)
    q_col = Q_CHUNKS[0] * CHUNK // LANES
    k_col = K_CHUNKS[0] * CHUNK // LANES
    v_col = KV_CHUNKS[2] * CHUNK // LANES

    for l in range(depth):
        last = l == depth - 1
        lam_init = 0.8 - 0.6 * math.exp(-0.3 * l)
        mod_lat = mod[l, :bsz].reshape(bsz, 6, 1, d)
        mod_ctx = mod[l, bsz:bsz + 1].reshape(1, 6, 1, d)
        lamp = jnp.zeros((8, LANES), F32)
        lamp = lamp.at[0, :DIFF_HEAD_DIM].set(lambda_q1[l]).at[1, :DIFF_HEAD_DIM].set(lambda_k1[l])
        lamp = lamp.at[2, :DIFF_HEAD_DIM].set(lambda_q2[l]).at[3, :DIFF_HEAD_DIM].set(lambda_k2[l])
        gsub = g_subln[l].reshape(1, LANES)

        p_lat = _in_proj(xl, g_pre_mix[l], mod_lat, w_in_b, l, cos, sin,
                         chunks=ALL_CHUNKS, group_rows=n, tm=512)
        ctx_chunks = KV_CHUNKS if last else ALL_CHUNKS
        p_ctx = _in_proj(xc, g_pre_mix[l], mod_ctx, w_in_b, l, None, None,
                         chunks=ctx_chunks, group_rows=bsz * n_ctx, tm=512)
        kc_col = ctx_chunks.index(K_CHUNKS[0]) * CHUNK // LANES
        vc_col = ctx_chunks.index(KV_CHUNKS[2]) * CHUNK // LANES

        ao = _attention(p_lat, q_col, [(p_ctx, kc_col, vc_col, n_ctx), (p_lat, k_col, v_col, n)],
                        lamp, gsub, batch=bsz, q_len=n, tq=n, lam_init=lam_init)
        yf = _fourier(p_lat, w_fourier, l, batch=bsz, seq_len=n, tl=512)
        yc = _conv(p_lat, w_dw[l], b_dw[l], g_conv_ln[l], b_conv_ln[l], w_pw_b, l, b_conv_pw[l],
                   batch=bsz, seq_len=n)
        xl_mid = _out_proj(yf, yc, ao, w_out_b, l, xl, mod_lat, g_post_mix[l], group_rows=n, tm=512)

        if not last:
            aoc = _attention(p_ctx, q_col, [(p_ctx, kc_col, vc_col, n_ctx)], lamp, gsub,
                             batch=bsz, q_len=n_ctx, tq=n_ctx, lam_init=lam_init)
            yfc = _fourier(p_ctx, w_fourier, l, batch=bsz, seq_len=n_ctx, tl=n_ctx)
            ycc = _conv(p_ctx, w_dw[l], b_dw[l], g_conv_ln[l], b_conv_ln[l], w_pw_b, l, b_conv_pw[l],
                        batch=bsz, seq_len=n_ctx)
            xc_mid = _out_proj(yfc, ycc, aoc, w_out_b, l, xc, mod_ctx, g_post_mix[l],
                               group_rows=bsz * n_ctx, tm=512)
            xc = _mlp(xc_mid, g_pre_mlp[l], mod_ctx, g_post_mlp[l], w_mlp_in, w_mlp_out, l,
                      group_rows=bsz * n_ctx, tm=1024, tf=512)

        xl = _mlp(xl_mid, g_pre_mlp[l], mod_lat, g_post_mlp[l], w_mlp_in, w_mlp_out, l,
                  group_rows=n, tm=1024, tf=512)

    return xl.reshape(bsz, n, d)
```

```python
import functools
import math

import numpy as np
import jax
import jax.numpy as jnp
from jax import lax
from jax.experimental import pallas as pl
from jax.experimental.pallas import tpu as pltpu

F32 = jnp.float32
BF16 = jnp.bfloat16

D_MODEL = 2048
GRID_W = 64
FOURIER_W = 512
FOURIER_GROUPS = 4
FOURIER_GW = 128
CONV_W = 512
CONV_K = 31
ATT_W = 1024
DIFF_HEAD_DIM = 64
DIFF_HEADS = 8
D_FF = 4 * D_MODEL
ROPE_BASE = 10000.0
EPS = 1e-6
IN_COLS = 4608

CHUNK = 512
Q_CHUNKS = (3, 4)
K_CHUNKS = (5, 6)
ALL_CHUNKS = tuple(range(IN_COLS // CHUNK))
KV_CHUNKS = (5, 6, 7, 8)

Q_SCALE = DIFF_HEAD_DIM ** -0.5 * math.log2(math.e)

LANES = 128
SUBLANES = 8
V7X_VMEM_LIMIT_BYTES = 56 * 1024 * 1024


def _params(*semantics):
    return pltpu.CompilerParams(dimension_semantics=semantics,
                                vmem_limit_bytes=V7X_VMEM_LIMIT_BYTES)


def _rms(y):
    return y * lax.rsqrt(jnp.mean(y * y, axis=-1, keepdims=True) + EPS)


def _layer_spec(w, layer, index_map_tail, block_tail=None, **kw):
    block_tail = w.shape[1:] if block_tail is None else block_tail
    return pl.BlockSpec((None,) + tuple(block_tail), lambda *g: (layer,) + tuple(index_map_tail(*g)), **kw)


def _adaln_kernel(c_ref, w_ref, b_ref, o_ref):
    s = jax.nn.silu(c_ref[...]).astype(BF16)
    o_ref[...] = jnp.dot(s, w_ref[...].astype(BF16), preferred_element_type=F32) + b_ref[...]


def _adaln(cvec, w_ada, b_ada):
    depth, d, n6 = w_ada.shape
    tn = 1024
    return pl.pallas_call(
        _adaln_kernel,
        grid=(depth, n6 // tn),
        in_specs=[pl.BlockSpec((8, d), lambda l, j: (0, 0)),
                  pl.BlockSpec((None, d, tn), lambda l, j: (l, 0, j)),
                  pl.BlockSpec((None, 1, tn), lambda l, j: (l, 0, j))],
        out_specs=pl.BlockSpec((None, 8, tn), lambda l, j: (l, 0, j)),
        out_shape=jax.ShapeDtypeStruct((depth, 8, n6), F32),
        compiler_params=_params("arbitrary", "arbitrary"),
        name="adaln",
    )(cvec, w_ada, b_ada.reshape(depth, 1, n6))


def _in_proj_kernel(*refs, chunks, rope):
    if rope:
        x_ref, g_ref, sh_ref, sc_ref, w_ref, cos_ref, sin_ref, o_ref = refs
    else:
        x_ref, g_ref, sh_ref, sc_ref, w_ref, o_ref = refs
    h = _rms(x_ref[...]) * g_ref[...]
    hb = (h * (1.0 + sc_ref[...]) + sh_ref[...]).astype(BF16)
    if rope:
        cos = cos_ref[...]
        sin = sin_ref[...]
        lane = lax.broadcasted_iota(jnp.int32, cos.shape, 1)
        first_half = (lane % DIFF_HEAD_DIM) < (DIFF_HEAD_DIM // 2)
    for n, j in enumerate(chunks):
        acc = jnp.dot(hb, w_ref[:, j * CHUNK:(j + 1) * CHUNK], preferred_element_type=F32)
        if j in Q_CHUNKS or j in K_CHUNKS:
            for s in range(CHUNK // LANES):
                t = acc[:, s * LANES:(s + 1) * LANES]
                if rope:
                    swapped = jnp.where(first_half,
                                        pltpu.roll(t, LANES - DIFF_HEAD_DIM // 2, 1),
                                        pltpu.roll(t, DIFF_HEAD_DIM // 2, 1))
                    t = t * cos + swapped * sin
                if j in Q_CHUNKS:
                    t = t * Q_SCALE
                o_ref[:, n * CHUNK + s * LANES:n * CHUNK + (s + 1) * LANES] = t.astype(BF16)
        else:
            o_ref[:, n * CHUNK:(n + 1) * CHUNK] = acc.astype(BF16)


def _in_proj(x, g_pre, modv, w, layer, cos, sin, *, chunks, group_rows, tm):
    m, d = x.shape
    bpg = group_rows // tm
    rope = cos is not None
    in_specs = [pl.BlockSpec((tm, d), lambda i: (i, 0)),
                pl.BlockSpec((1, d), lambda i: (0, 0)),
                pl.BlockSpec((None, None, 1, d), lambda i: (i // bpg, 0, 0, 0)),
                pl.BlockSpec((None, None, 1, d), lambda i: (i // bpg, 1, 0, 0)),
                _layer_spec(w, layer, lambda i: (0, 0), pipeline_mode=pl.Buffered(1))]
    args = [x, g_pre.reshape(1, d), modv, modv, w]
    if rope:
        in_specs += [pl.BlockSpec((tm, LANES), lambda i: (i % bpg, 0)),
                     pl.BlockSpec((tm, LANES), lambda i: (i % bpg, 0))]
        args += [cos, sin]
    ncol = len(chunks) * CHUNK
    return pl.pallas_call(
        functools.partial(_in_proj_kernel, chunks=chunks, rope=rope),
        grid=(m // tm,),
        in_specs=in_specs,
        out_specs=pl.BlockSpec((tm, ncol), lambda i: (i, 0)),
        out_shape=jax.ShapeDtypeStruct((m, ncol), BF16),
        compiler_params=_params("arbitrary"),
        name="in_proj",
    )(*args)


ATT_UNIT_ROWS = 256


def _attn_kernel(*refs, kv_lens, lam_init):
    n_kv = len(kv_lens)
    q_ref = refs[0]
    kv_refs = refs[1:1 + 2 * n_kv]
    lamp_ref, gsub_ref, o_ref, kt_scr, v_scr = refs[1 + 2 * n_kv:]

    @pl.when(pl.program_id(2) == 0)
    def _():
        off = 0
        for i, n in enumerate(kv_lens):
            kt_scr[:, off:off + n] = kv_refs[2 * i][...].T
            v_scr[off:off + n, 0:LANES] = kv_refs[2 * i + 1][...]
            off += n
        v_scr[:, LANES:2 * LANES] = jnp.ones((off, LANES), BF16)

    lp = lamp_ref[...]
    lam = (jnp.exp(jnp.sum(lp[0:1] * lp[1:2], axis=-1, keepdims=True))
           - jnp.exp(jnp.sum(lp[2:3] * lp[3:4], axis=-1, keepdims=True)) + lam_init)
    rows = ATT_UNIT_ROWS
    n_units = q_ref.shape[0] // rows

    def scores(u):
        q = q_ref[u * rows:(u + 1) * rows, :]
        lane = lax.broadcasted_iota(jnp.int32, q.shape, 1)
        zero = jnp.zeros_like(q)
        qq = jnp.concatenate([jnp.where(lane < DIFF_HEAD_DIM, q, zero),
                              jnp.where(lane < DIFF_HEAD_DIM, zero, q)], axis=0)
        return jnp.dot(qq, kt_scr[...], preferred_element_type=F32)

    s_next = scores(0)
    for u in range(n_units):
        s = s_next
        if u + 1 < n_units:
            s_next = scores(u + 1)
        e = jnp.exp2(s - s.max(axis=-1, keepdims=True)).astype(BF16)
        acc1 = jnp.dot(e[:rows], v_scr[...], preferred_element_type=F32)
        acc2 = jnp.dot(e[rows:], v_scr[...], preferred_element_type=F32)
        o = (acc1[:, 0:LANES] / acc1[:, LANES:2 * LANES]
             - lam * (acc2[:, 0:LANES] / acc2[:, LANES:2 * LANES]))
        o_ref[u * rows:(u + 1) * rows, :] = (_rms(o) * gsub_ref[...] * (1.0 - lam_init)).astype(BF16)


def _attention(q_arr, q_col, kvs, lamp, gsub, *, batch, q_len, tq, lam_init):
    nqb = q_len // tq
    in_specs = [pl.BlockSpec((tq, LANES), lambda b, h, i: (b * nqb + i, q_col + h))]
    args = [q_arr]
    for arr, k_col, v_col, kv_len in kvs:
        in_specs.append(pl.BlockSpec((kv_len, LANES), lambda b, h, i, c=k_col: (b, c + h)))
        in_specs.append(pl.BlockSpec((kv_len, LANES), lambda b, h, i, c=v_col: (b, c + h)))
        args += [arr, arr]
    in_specs += [pl.BlockSpec(lamp.shape, lambda b, h, i: (0, 0)),
                 pl.BlockSpec(gsub.shape, lambda b, h, i: (0, 0))]
    args += [lamp, gsub]
    kv_lens = tuple(kv[3] for kv in kvs)
    return pl.pallas_call(
        functools.partial(_attn_kernel, kv_lens=kv_lens, lam_init=lam_init),
        grid=(batch, DIFF_HEADS, nqb),
        in_specs=in_specs,
        out_specs=pl.BlockSpec((tq, LANES), lambda b, h, i: (b * nqb + i, h)),
        out_shape=jax.ShapeDtypeStruct((batch * q_len, ATT_W), BF16),
        scratch_shapes=[pltpu.VMEM((LANES, sum(kv_lens)), BF16),
                        pltpu.VMEM((sum(kv_lens), 2 * LANES), BF16)],
        compiler_params=_params("arbitrary", "arbitrary", "arbitrary"),
        name="diff_attn",
    )(*args)


FOURIER_BLK = 128


def _dft_constants(seq_len):
    half = seq_len // 2
    k = np.arange(half, dtype=np.int64)
    ang = 2.0 * np.pi * ((k[:, None] * k[None, :]) % seq_len).astype(np.float64) / seq_len
    ch = np.cos(ang).astype(np.float32)
    sh = np.sin(ang).astype(np.float32)
    c = np.arange(FOURIER_GW, dtype=np.int64)
    angc = 2.0 * np.pi * ((c[:, None] * c[None, :]) % FOURIER_GW).astype(np.float64) / FOURIER_GW
    norm = 1.0 / math.sqrt(seq_len * FOURIER_GW)
    cc = (np.cos(angc) * norm).astype(np.float32)
    sc = (np.sin(angc) * norm).astype(np.float32)
    perm = np.zeros((FOURIER_BLK, 2 * FOURIER_BLK), np.float32)
    r = np.arange(1, FOURIER_BLK)
    perm[r, FOURIER_BLK - r] = 1.0
    perm[0, FOURIER_BLK] = 1.0
    alt = np.zeros((SUBLANES, half), np.float32)
    alt[0] = 1.0 - 2.0 * (k % 2)
    return tuple(jnp.asarray(t).astype(BF16) for t in (ch, sh, cc, sc, perm, alt))


def _fourier_kernel(u_ref, wf_ref, cc_ref, sc_ref, ch_ref, sh_ref, perm_ref, alt_ref, o_ref, r1_scr, r2_scr):
    seq_len = u_ref.shape[0]
    half = seq_len // 2
    blk = FOURIER_BLK
    nb = half // blk
    groups = [slice(g * FOURIER_GW, (g + 1) * FOURIER_GW) for g in range(FOURIER_GROUPS)]
    perm = perm_ref[...]
    rev = perm[:, 0:blk]
    mix_a, mix_b = [], []
    for g in groups:
        wf = wf_ref[g.start // FOURIER_GW].astype(BF16)
        mix_a.append(jnp.dot(cc_ref[...], wf, preferred_element_type=F32).astype(BF16))
        mix_b.append(jnp.dot(sc_ref[...], wf, preferred_element_type=F32).astype(BF16))

    for i in range(nb):
        if i == 0:
            u_rev = jnp.dot(rev, u_ref[seq_len - blk:seq_len, :], preferred_element_type=F32)
        else:
            u_rev = jnp.dot(perm, u_ref[seq_len - (i + 1) * blk:seq_len - (i - 1) * blk, :],
                            preferred_element_type=F32)
        u_blk = u_ref[i * blk:(i + 1) * blk, :].astype(F32)
        ue = (u_blk + u_rev).astype(BF16)
        uo = (u_blk - u_rev).astype(BF16)
        for g, a, b in zip(groups, mix_a, mix_b):
            r1_scr[i * blk:(i + 1) * blk, g] = jnp.dot(ue[:, g], a, preferred_element_type=F32).astype(BF16)
            r2_scr[i * blk:(i + 1) * blk, g] = jnp.dot(uo[:, g], b, preferred_element_type=F32).astype(BF16)

    u_mid = u_ref[half:half + SUBLANES, :]
    t_mid = jnp.concatenate([jnp.dot(u_mid[:, g], a, preferred_element_type=F32)
                             for g, a in zip(groups, mix_a)], axis=1)[0:1]
    k = lax.broadcasted_iota(jnp.int32, (half, 1), 0)
    sign = jnp.where(k % 2 == 0, 1.0, -1.0).astype(F32)
    p = jnp.dot(ch_ref[...], r1_scr[...], preferred_element_type=F32) + sign * t_mid
    q = jnp.dot(sh_ref[...], r2_scr[...], preferred_element_type=F32)
    o_ref[0:half, :] = (p - q).astype(BF16)
    z = (p + q).astype(BF16)
    y_mid = jnp.dot(alt_ref[...], r1_scr[...], preferred_element_type=F32)[0:1] + t_mid
    row = lax.broadcasted_iota(jnp.int32, (blk, 1), 0)
    for i in range(nb):
        if i == 0:
            w = jnp.dot(rev, z[half - blk:half], preferred_element_type=F32)
            w = jnp.where(row == 0, y_mid, w)
        else:
            w = jnp.dot(perm, z[half - (i + 1) * blk:half - (i - 1) * blk], preferred_element_type=F32)
        o_ref[half + i * blk:half + (i + 1) * blk, :] = w.astype(BF16)


def _fourier(p, w_f, layer, *, batch, seq_len):
    ch, sh, cc, sc, perm, alt = _dft_constants(seq_len)
    half = seq_len // 2
    full = lambda t: pl.BlockSpec(t.shape, lambda b: (0,) * t.ndim)
    return pl.pallas_call(
        _fourier_kernel,
        grid=(batch,),
        in_specs=[pl.BlockSpec((seq_len, FOURIER_W), lambda b: (b, 0)),
                  _layer_spec(w_f, layer, lambda b: (0, 0, 0)),
                  full(cc), full(sc), full(ch), full(sh), full(perm), full(alt)],
        out_specs=pl.BlockSpec((seq_len, FOURIER_W), lambda b: (b, 0)),
        out_shape=jax.ShapeDtypeStruct((batch * seq_len, FOURIER_W), BF16),
        scratch_shapes=[pltpu.VMEM((half, FOURIER_W), BF16), pltpu.VMEM((half, FOURIER_W), BF16)],
        compiler_params=_params("arbitrary"),
        name="fourier",
    )(p, w_f, cc, sc, ch, sh, perm, alt)


CONV_PAD = 16
CONV_ROWS = 128
CONV_LEAD = CONV_PAD - CONV_K // 2


def _conv_kernel(a_ref, gt_ref, wdw_ref, bdw_ref, gln_ref, bln_ref, wpw_ref, bpw_ref, o_ref, zp_scr):
    seq_len = a_ref.shape[0]
    zp_scr[0:CONV_PAD, :] = jnp.zeros((CONV_PAD, CONV_W), F32)
    zp_scr[CONV_PAD + seq_len:2 * CONV_PAD + seq_len, :] = jnp.zeros((CONV_PAD, CONV_W), F32)
    zp_scr[CONV_PAD:CONV_PAD + seq_len, :] = (a_ref[...].astype(F32)
                                              * jax.nn.sigmoid(gt_ref[...].astype(F32)))
    span = CONV_ROWS + 2 * CONV_PAD - SUBLANES

    def body(r, carry):
        base = pl.multiple_of(r * CONV_ROWS, CONV_ROWS)
        cols = []
        for c in range(CONV_W // LANES):
            win = zp_scr[pl.ds(base, CONV_ROWS + 2 * CONV_PAD), c * LANES:(c + 1) * LANES]
            acc = None
            for shift in range(SUBLANES):
                shifted = win if shift == 0 else pltpu.roll(win, win.shape[0] - shift, 0)
                for a in range(span // SUBLANES):
                    k = SUBLANES * a + shift - CONV_LEAD
                    if 0 <= k < CONV_K and SUBLANES * a + CONV_ROWS <= span:
                        term = (shifted[SUBLANES * a:SUBLANES * a + CONV_ROWS]
                                * wdw_ref[k:k + 1, c * LANES:(c + 1) * LANES])
                        acc = term if acc is None else acc + term
            cols.append(acc)
        z = jnp.concatenate(cols, axis=1) + bdw_ref[...]
        zc = z - jnp.mean(z, axis=-1, keepdims=True)
        y = zc * lax.rsqrt(jnp.mean(zc * zc, axis=-1, keepdims=True) + EPS) * gln_ref[...] + bln_ref[...]
        s = (y * jax.nn.sigmoid(y)).astype(BF16)
        out = jnp.dot(s, wpw_ref[...], preferred_element_type=F32) + bpw_ref[...]
        o_ref[pl.ds(base, CONV_ROWS), :] = out.astype(BF16)
        return carry

    lax.fori_loop(0, seq_len // CONV_ROWS, body, 0)


def _conv(p, w_dw, b_dw, g_ln, b_ln, w_pw, layer, b_pw, *, batch, seq_len):
    vec = pl.BlockSpec((1, CONV_W), lambda b: (0, 0))
    return pl.pallas_call(
        _conv_kernel,
        grid=(batch,),
        in_specs=[pl.BlockSpec((seq_len, CONV_W), lambda b: (b, 1)),
                  pl.BlockSpec((seq_len, CONV_W), lambda b: (b, 2)),
                  pl.BlockSpec(w_dw.shape, lambda b: (0, 0)),
                  vec, vec, vec,
                  _layer_spec(w_pw, layer, lambda b: (0, 0)),
                  vec],
        out_specs=pl.BlockSpec((seq_len, CONV_W), lambda b: (b, 0)),
        out_shape=jax.ShapeDtypeStruct((batch * seq_len, CONV_W), BF16),
        scratch_shapes=[pltpu.VMEM((seq_len + 2 * CONV_PAD, CONV_W), F32)],
        compiler_params=_params("arbitrary"),
        name="conv_module",
    )(p, p, w_dw, b_dw.reshape(1, -1), g_ln.reshape(1, -1), b_ln.reshape(1, -1), w_pw, b_pw.reshape(1, -1))


def _out_proj_kernel(yf_ref, yc_ref, ao_ref, w_ref, x_ref, gate_ref, gpost_ref, o_ref):
    y = jnp.dot(yf_ref[...], w_ref[0:FOURIER_W, :], preferred_element_type=F32)
    y = y + jnp.dot(yc_ref[...], w_ref[FOURIER_W:FOURIER_W + CONV_W, :], preferred_element_type=F32)
    y = y + jnp.dot(ao_ref[...], w_ref[FOURIER_W + CONV_W:, :], preferred_element_type=F32)
    o_ref[...] = x_ref[...] + gate_ref[...] * (_rms(y) * gpost_ref[...])


def _out_proj(yf, yc, ao, w, layer, x, modv, g_post, *, group_rows, tm):
    m, d = x.shape
    bpg = group_rows // tm
    return pl.pallas_call(
        _out_proj_kernel,
        grid=(m // tm,),
        in_specs=[pl.BlockSpec((tm, FOURIER_W), lambda i: (i, 0)),
                  pl.BlockSpec((tm, CONV_W), lambda i: (i, 0)),
                  pl.BlockSpec((tm, ATT_W), lambda i: (i, 0)),
                  _layer_spec(w, layer, lambda i: (0, 0), pipeline_mode=pl.Buffered(1)),
                  pl.BlockSpec((tm, d), lambda i: (i, 0)),
                  pl.BlockSpec((None, None, 1, d), lambda i: (i // bpg, 2, 0, 0)),
                  pl.BlockSpec((1, d), lambda i: (0, 0))],
        out_specs=pl.BlockSpec((tm, d), lambda i: (i, 0)),
        out_shape=jax.ShapeDtypeStruct((m, d), F32),
        compiler_params=_params("arbitrary"),
        name="out_proj",
    )(yf, yc, ao, w, x, modv, g_post.reshape(1, d))


def _mlp_kernel(x_ref, g_ref, sh_ref, sc_ref, gate_ref, gpost_ref, w1_ref, w2_ref, o_ref, h_scr):
    j = pl.program_id(1)
    last = pl.num_programs(1) - 1

    def ff_chunk(hb):
        h1 = jnp.maximum(jnp.dot(hb, w1_ref[...].astype(BF16), preferred_element_type=F32), 0.0)
        return jnp.dot((h1 * h1).astype(BF16), w2_ref[...].astype(BF16), preferred_element_type=F32)

    @pl.when(j == 0)
    def _():
        h = _rms(x_ref[...]) * g_ref[...]
        hb = (h * (1.0 + sc_ref[...]) + sh_ref[...]).astype(BF16)
        h_scr[...] = hb
        o_ref[...] = ff_chunk(hb)

    @pl.when(jnp.logical_and(j > 0, j < last))
    def _():
        o_ref[...] += ff_chunk(h_scr[...])

    @pl.when(j == last)
    def _():
        y = o_ref[...] + ff_chunk(h_scr[...])
        o_ref[...] = x_ref[...] + gate_ref[...] * (_rms(y) * gpost_ref[...])


def _mlp(x, g_pre, modv, g_post, w1, w2, layer, *, group_rows, tm, tf):
    m, d = x.shape
    dff = w1.shape[-1]
    bpg = group_rows // tm
    mod_spec = lambda k: pl.BlockSpec((None, None, 1, d), lambda i, j: (i // bpg, k, 0, 0))
    return pl.pallas_call(
        _mlp_kernel,
        grid=(m // tm, dff // tf),
        in_specs=[pl.BlockSpec((tm, d), lambda i, j: (i, 0), pipeline_mode=pl.Buffered(1)),
                  pl.BlockSpec((1, d), lambda i, j: (0, 0)),
                  mod_spec(3), mod_spec(4), mod_spec(5),
                  pl.BlockSpec((1, d), lambda i, j: (0, 0)),
                  _layer_spec(w1, layer, lambda i, j: (0, j), block_tail=(d, tf)),
                  _layer_spec(w2, layer, lambda i, j: (j, 0), block_tail=(tf, d))],
        out_specs=pl.BlockSpec((tm, d), lambda i, j: (i, 0)),
        out_shape=jax.ShapeDtypeStruct((m, d), F32),
        scratch_shapes=[pltpu.VMEM((tm, d), BF16)],
        compiler_params=_params("arbitrary", "arbitrary"),
        name="mlp",
    )(x, g_pre.reshape(1, d), modv, modv, modv, g_post.reshape(1, d), w1, w2)


def _rope_tables(n):
    rows = n // GRID_W
    row = jnp.repeat(jnp.arange(rows), GRID_W).astype(F32)
    col = jnp.tile(jnp.arange(GRID_W), rows).astype(F32)
    n_freq = DIFF_HEAD_DIM // 4
    inv = ROPE_BASE ** (-jnp.arange(n_freq, dtype=F32) / n_freq)
    ang = jnp.concatenate([row[:, None] * inv, col[:, None] * inv], axis=-1)
    cos, sin = jnp.cos(ang), jnp.sin(ang)
    return (jnp.concatenate([cos, cos, cos, cos], axis=-1),
            jnp.concatenate([-sin, sin, -sin, sin], axis=-1))


def kernel(x, c, ctx, c_ctx, w_ada, b_ada, g_pre_mix, g_post_mix, g_pre_mlp, g_post_mlp, w_in, w_out, w_fourier, w_dw, b_dw, g_conv_ln, b_conv_ln, w_conv_pw, b_conv_pw, lambda_q1, lambda_k1, lambda_q2, lambda_k2, g_subln, w_mlp_in, w_mlp_out):
    bsz, n, d = x.shape
    n_ctx = ctx.shape[1]
    depth = w_ada.shape[0]

    cvec = jnp.zeros((8, d), F32).at[:bsz].set(c).at[bsz].set(c_ctx)
    mod = _adaln(cvec, w_ada, b_ada)

    w_in_b = w_in.astype(BF16)
    w_out_b = w_out.astype(BF16)
    w_pw_b = w_conv_pw.astype(BF16)
    cos, sin = _rope_tables(n)

    xl = x.reshape(bsz * n, d)
    xc = ctx.reshape(bsz * n_ctx, d)
    q_col = Q_CHUNKS[0] * CHUNK // LANES
    k_col = K_CHUNKS[0] * CHUNK // LANES
    v_col = KV_CHUNKS[2] * CHUNK // LANES

    for l in range(depth):
        last = l == depth - 1
        lam_init = 0.8 - 0.6 * math.exp(-0.3 * l)
        mod_lat = mod[l, :bsz].reshape(bsz, 6, 1, d)
        mod_ctx = mod[l, bsz:bsz + 1].reshape(1, 6, 1, d)
        lamp = jnp.zeros((8, LANES), F32)
        lamp = lamp.at[0, :DIFF_HEAD_DIM].set(lambda_q1[l]).at[1, :DIFF_HEAD_DIM].set(lambda_k1[l])
        lamp = lamp.at[2, :DIFF_HEAD_DIM].set(lambda_q2[l]).at[3, :DIFF_HEAD_DIM].set(lambda_k2[l])
        gsub = g_subln[l].reshape(1, LANES)

        p_lat = _in_proj(xl, g_pre_mix[l], mod_lat, w_in_b, l, cos, sin,
                         chunks=ALL_CHUNKS, group_rows=n, tm=512)
        ctx_chunks = KV_CHUNKS if last else ALL_CHUNKS
        p_ctx = _in_proj(xc, g_pre_mix[l], mod_ctx, w_in_b, l, None, None,
                         chunks=ctx_chunks, group_rows=bsz * n_ctx, tm=512)
        kc_col = ctx_chunks.index(K_CHUNKS[0]) * CHUNK // LANES
        vc_col = ctx_chunks.index(KV_CHUNKS[2]) * CHUNK // LANES

        ao = _attention(p_lat, q_col, [(p_ctx, kc_col, vc_col, n_ctx), (p_lat, k_col, v_col, n)],
                        lamp, gsub, batch=bsz, q_len=n, tq=n, lam_init=lam_init)
        yf = _fourier(p_lat, w_fourier, l, batch=bsz, seq_len=n)
        yc = _conv(p_lat, w_dw[l], b_dw[l], g_conv_ln[l], b_conv_ln[l], w_pw_b, l, b_conv_pw[l],
                   batch=bsz, seq_len=n)
        xl_mid = _out_proj(yf, yc, ao, w_out_b, l, xl, mod_lat, g_post_mix[l], group_rows=n, tm=512)

        if not last:
            aoc = _attention(p_ctx, q_col, [(p_ctx, kc_col, vc_col, n_ctx)], lamp, gsub,
                             batch=bsz, q_len=n_ctx, tq=n_ctx, lam_init=lam_init)
            yfc = _fourier(p_ctx, w_fourier, l, batch=bsz, seq_len=n_ctx)
            ycc = _conv(p_ctx, w_dw[l], b_dw[l], g_conv_ln[l], b_conv_ln[l], w_pw_b, l, b_conv_pw[l],
                        batch=bsz, seq_len=n_ctx)
            xc_mid = _out_proj(yfc, ycc, aoc, w_out_b, l, xc, mod_ctx, g_post_mix[l],
                               group_rows=bsz * n_ctx, tm=512)
            xc = _mlp(xc_mid, g_pre_mlp[l], mod_ctx, g_post_mlp[l], w_mlp_in, w_mlp_out, l,
                      group_rows=bsz * n_ctx, tm=1024, tf=512)

        xl = _mlp(xl_mid, g_pre_mlp[l], mod_lat, g_post_mlp[l], w_mlp_in, w_mlp_out, l,
                  group_rows=n, tm=1024, tf=512)

    return xl.reshape(bsz, n, d)
```

```python
import functools
import math

import numpy as np
import jax
import jax.numpy as jnp
from jax import lax
from jax.experimental import pallas as pl
from jax.experimental.pallas import tpu as pltpu

F32 = jnp.float32
BF16 = jnp.bfloat16

D_MODEL = 2048
GRID_W = 64
FOURIER_W = 512
FOURIER_GROUPS = 4
FOURIER_GW = 128
CONV_W = 512
CONV_K = 31
ATT_W = 1024
DIFF_HEAD_DIM = 64
DIFF_HEADS = 8
D_FF = 4 * D_MODEL
ROPE_BASE = 10000.0
EPS = 1e-6
IN_COLS = 4608

CHUNK = 512
Q_CHUNKS = (3, 4)
K_CHUNKS = (5, 6)
ALL_CHUNKS = tuple(range(IN_COLS // CHUNK))
KV_CHUNKS = (5, 6, 7, 8)

Q_SCALE = DIFF_HEAD_DIM ** -0.5 * math.log2(math.e)

LANES = 128
SUBLANES = 8
V7X_VMEM_LIMIT_BYTES = 56 * 1024 * 1024
V7X_VMEM_LIMIT_MLP_BYTES = 62 * 1024 * 1024


def _params(*semantics):
    return pltpu.CompilerParams(dimension_semantics=semantics,
                                vmem_limit_bytes=V7X_VMEM_LIMIT_BYTES)


def _rms(y):
    return y * lax.rsqrt(jnp.mean(y * y, axis=-1, keepdims=True) + EPS)


def _layer_spec(w, layer, index_map_tail, block_tail=None, **kw):
    block_tail = w.shape[1:] if block_tail is None else block_tail
    return pl.BlockSpec((None,) + tuple(block_tail), lambda *g: (layer,) + tuple(index_map_tail(*g)), **kw)


def _adaln_kernel(c_ref, w_ref, b_ref, o_ref):
    s = jax.nn.silu(c_ref[...]).astype(BF16)
    o_ref[...] = jnp.dot(s, w_ref[...].astype(BF16), preferred_element_type=F32) + b_ref[...]


def _adaln(cvec, w_ada, b_ada):
    depth, d, n6 = w_ada.shape
    tn = 1024
    return pl.pallas_call(
        _adaln_kernel,
        grid=(depth, n6 // tn),
        in_specs=[pl.BlockSpec((8, d), lambda l, j: (0, 0)),
                  pl.BlockSpec((None, d, tn), lambda l, j: (l, 0, j)),
                  pl.BlockSpec((None, 1, tn), lambda l, j: (l, 0, j))],
        out_specs=pl.BlockSpec((None, 8, tn), lambda l, j: (l, 0, j)),
        out_shape=jax.ShapeDtypeStruct((depth, 8, n6), F32),
        compiler_params=_params("arbitrary", "arbitrary"),
        name="adaln",
    )(cvec, w_ada, b_ada.reshape(depth, 1, n6))


def _in_proj_kernel(*refs, chunks, rope):
    if rope:
        x_ref, g_ref, sh_ref, sc_ref, w_ref, cos_ref, sin_ref, o_ref = refs
    else:
        x_ref, g_ref, sh_ref, sc_ref, w_ref, o_ref = refs
    h = _rms(x_ref[...]) * g_ref[...]
    hb = (h * (1.0 + sc_ref[...]) + sh_ref[...]).astype(BF16)
    if rope:
        cos = cos_ref[...]
        sin = sin_ref[...]
        lane = lax.broadcasted_iota(jnp.int32, cos.shape, 1)
        first_half = (lane % DIFF_HEAD_DIM) < (DIFF_HEAD_DIM // 2)
    for n, j in enumerate(chunks):
        acc = jnp.dot(hb, w_ref[:, j * CHUNK:(j + 1) * CHUNK], preferred_element_type=F32)
        if j in Q_CHUNKS or j in K_CHUNKS:
            for s in range(CHUNK // LANES):
                t = acc[:, s * LANES:(s + 1) * LANES]
                if rope:
                    swapped = jnp.where(first_half,
                                        pltpu.roll(t, LANES - DIFF_HEAD_DIM // 2, 1),
                                        pltpu.roll(t, DIFF_HEAD_DIM // 2, 1))
                    t = t * cos + swapped * sin
                if j in Q_CHUNKS:
                    t = t * Q_SCALE
                o_ref[:, n * CHUNK + s * LANES:n * CHUNK + (s + 1) * LANES] = t.astype(BF16)
        else:
            o_ref[:, n * CHUNK:(n + 1) * CHUNK] = acc.astype(BF16)


def _in_proj(x, g_pre, modv, w, layer, cos, sin, *, chunks, group_rows, tm):
    m, d = x.shape
    bpg = group_rows // tm
    rope = cos is not None
    in_specs = [pl.BlockSpec((tm, d), lambda i: (i, 0)),
                pl.BlockSpec((1, d), lambda i: (0, 0)),
                pl.BlockSpec((None, None, 1, d), lambda i: (i // bpg, 0, 0, 0)),
                pl.BlockSpec((None, None, 1, d), lambda i: (i // bpg, 1, 0, 0)),
                _layer_spec(w, layer, lambda i: (0, 0), pipeline_mode=pl.Buffered(1))]
    args = [x, g_pre.reshape(1, d), modv, modv, w]
    if rope:
        in_specs += [pl.BlockSpec((tm, LANES), lambda i: (i % bpg, 0)),
                     pl.BlockSpec((tm, LANES), lambda i: (i % bpg, 0))]
        args += [cos, sin]
    ncol = len(chunks) * CHUNK
    return pl.pallas_call(
        functools.partial(_in_proj_kernel, chunks=chunks, rope=rope),
        grid=(m // tm,),
        in_specs=in_specs,
        out_specs=pl.BlockSpec((tm, ncol), lambda i: (i, 0)),
        out_shape=jax.ShapeDtypeStruct((m, ncol), BF16),
        compiler_params=_params("arbitrary"),
        name="in_proj",
    )(*args)


ATT_UNIT_ROWS = 512


def _attn_kernel(*refs, kv_lens, lam_init):
    n_kv = len(kv_lens)
    q_ref = refs[0]
    kv_refs = refs[1:1 + 2 * n_kv]
    lamp_ref, gsub_ref, o_ref, kt_scr, v_scr = refs[1 + 2 * n_kv:]

    @pl.when(pl.program_id(2) == 0)
    def _():
        off = 0
        for i, n in enumerate(kv_lens):
            kt_scr[:, off:off + n] = kv_refs[2 * i][...].T
            v_scr[off:off + n, 0:LANES] = kv_refs[2 * i + 1][...]
            off += n
        v_scr[:, LANES:2 * LANES] = jnp.ones((off, LANES), BF16)

    lp = lamp_ref[...]
    lam = (jnp.exp(jnp.sum(lp[0:1] * lp[1:2], axis=-1, keepdims=True))
           - jnp.exp(jnp.sum(lp[2:3] * lp[3:4], axis=-1, keepdims=True)) + lam_init)
    rows = min(ATT_UNIT_ROWS, q_ref.shape[0])
    n_units = q_ref.shape[0] // rows

    def scores(u):
        q = q_ref[u * rows:(u + 1) * rows, :]
        lane = lax.broadcasted_iota(jnp.int32, q.shape, 1)
        zero = jnp.zeros_like(q)
        qq = jnp.concatenate([jnp.where(lane < DIFF_HEAD_DIM, q, zero),
                              jnp.where(lane < DIFF_HEAD_DIM, zero, q)], axis=0)
        return jnp.dot(qq, kt_scr[...], preferred_element_type=F32)

    s_next = scores(0)
    for u in range(n_units):
        s = s_next
        if u + 1 < n_units:
            s_next = scores(u + 1)
        e = jnp.exp2(s - s.max(axis=-1, keepdims=True)).astype(BF16)
        acc1 = jnp.dot(e[:rows], v_scr[...], preferred_element_type=F32)
        acc2 = jnp.dot(e[rows:], v_scr[...], preferred_element_type=F32)
        o = (acc1[:, 0:LANES] / acc1[:, LANES:2 * LANES]
             - lam * (acc2[:, 0:LANES] / acc2[:, LANES:2 * LANES]))
        o_ref[u * rows:(u + 1) * rows, :] = (_rms(o) * gsub_ref[...] * (1.0 - lam_init)).astype(BF16)


def _attention(q_arr, q_col, kvs, lamp, gsub, *, batch, q_len, tq, lam_init):
    nqb = q_len // tq
    in_specs = [pl.BlockSpec((tq, LANES), lambda b, h, i: (b * nqb + i, q_col + h))]
    args = [q_arr]
    for arr, k_col, v_col, kv_len in kvs:
        in_specs.append(pl.BlockSpec((kv_len, LANES), lambda b, h, i, c=k_col: (b, c + h)))
        in_specs.append(pl.BlockSpec((kv_len, LANES), lambda b, h, i, c=v_col: (b, c + h)))
        args += [arr, arr]
    in_specs += [pl.BlockSpec(lamp.shape, lambda b, h, i: (0, 0)),
                 pl.BlockSpec(gsub.shape, lambda b, h, i: (0, 0))]
    args += [lamp, gsub]
    kv_lens = tuple(kv[3] for kv in kvs)
    return pl.pallas_call(
        functools.partial(_attn_kernel, kv_lens=kv_lens, lam_init=lam_init),
        grid=(batch, DIFF_HEADS, nqb),
        in_specs=in_specs,
        out_specs=pl.BlockSpec((tq, LANES), lambda b, h, i: (b * nqb + i, h)),
        out_shape=jax.ShapeDtypeStruct((batch * q_len, ATT_W), BF16),
        scratch_shapes=[pltpu.VMEM((LANES, sum(kv_lens)), BF16),
                        pltpu.VMEM((sum(kv_lens), 2 * LANES), BF16)],
        compiler_params=_params("arbitrary", "arbitrary", "arbitrary"),
        name="diff_attn",
    )(*args)


FOURIER_BLK = 128


def _dft_constants(seq_len):
    half = seq_len // 2
    k = np.arange(half, dtype=np.int64)
    ang = 2.0 * np.pi * ((k[:, None] * k[None, :]) % seq_len).astype(np.float64) / seq_len
    ch = np.cos(ang).astype(np.float32)
    sh = np.sin(ang).astype(np.float32)
    c = np.arange(FOURIER_GW, dtype=np.int64)
    angc = 2.0 * np.pi * ((c[:, None] * c[None, :]) % FOURIER_GW).astype(np.float64) / FOURIER_GW
    norm = 1.0 / math.sqrt(seq_len * FOURIER_GW)
    cc = (np.cos(angc) * norm).astype(np.float32)
    sc = (np.sin(angc) * norm).astype(np.float32)
    perm = np.zeros((FOURIER_BLK, 2 * FOURIER_BLK), np.float32)
    r = np.arange(1, FOURIER_BLK)
    perm[r, FOURIER_BLK - r] = 1.0
    perm[0, FOURIER_BLK] = 1.0
    alt = np.zeros((SUBLANES, half), np.float32)
    alt[0] = 1.0 - 2.0 * (k % 2)
    return tuple(jnp.asarray(t).astype(BF16) for t in (ch, sh, cc, sc, perm, alt))


def _fourier_kernel(u_ref, wf_ref, cc_ref, sc_ref, ch_ref, sh_ref, perm_ref, alt_ref, o_ref, r1_scr, r2_scr):
    seq_len = u_ref.shape[0]
    half = seq_len // 2
    blk = FOURIER_BLK
    nb = half // blk
    groups = [slice(g * FOURIER_GW, (g + 1) * FOURIER_GW) for g in range(FOURIER_GROUPS)]
    perm = perm_ref[...]
    rev = perm[:, 0:blk]
    mix_a, mix_b = [], []
    for g in groups:
        wf = wf_ref[g.start // FOURIER_GW].astype(BF16)
        mix_a.append(jnp.dot(cc_ref[...], wf, preferred_element_type=F32).astype(BF16))
        mix_b.append(jnp.dot(sc_ref[...], wf, preferred_element_type=F32).astype(BF16))

    for i in range(nb):
        if i == 0:
            u_rev = jnp.dot(rev, u_ref[seq_len - blk:seq_len, :], preferred_element_type=F32)
        else:
            u_rev = jnp.dot(perm, u_ref[seq_len - (i + 1) * blk:seq_len - (i - 1) * blk, :],
                            preferred_element_type=F32)
        u_blk = u_ref[i * blk:(i + 1) * blk, :].astype(F32)
        ue = (u_blk + u_rev).astype(BF16)
        uo = (u_blk - u_rev).astype(BF16)
        for g, a, b in zip(groups, mix_a, mix_b):
            r1_scr[i * blk:(i + 1) * blk, g] = jnp.dot(ue[:, g], a, preferred_element_type=F32).astype(BF16)
            r2_scr[i * blk:(i + 1) * blk, g] = jnp.dot(uo[:, g], b, preferred_element_type=F32).astype(BF16)

    u_mid = u_ref[half:half + SUBLANES, :]
    t_mid = jnp.concatenate([jnp.dot(u_mid[:, g], a, preferred_element_type=F32)
                             for g, a in zip(groups, mix_a)], axis=1)[0:1]
    k = lax.broadcasted_iota(jnp.int32, (half, 1), 0)
    sign = jnp.where(k % 2 == 0, 1.0, -1.0).astype(F32)
    p = jnp.dot(ch_ref[...], r1_scr[...], preferred_element_type=F32) + sign * t_mid
    q = jnp.dot(sh_ref[...], r2_scr[...], preferred_element_type=F32)
    o_ref[0:half, :] = (p - q).astype(BF16)
    z = (p + q).astype(BF16)
    y_mid = jnp.dot(alt_ref[...], r1_scr[...], preferred_element_type=F32)[0:1] + t_mid
    row = lax.broadcasted_iota(jnp.int32, (blk, 1), 0)
    for i in range(nb):
        if i == 0:
            w = jnp.dot(rev, z[half - blk:half], preferred_element_type=F32)
            w = jnp.where(row == 0, y_mid, w)
        else:
            w = jnp.dot(perm, z[half - (i + 1) * blk:half - (i - 1) * blk], preferred_element_type=F32)
        o_ref[half + i * blk:half + (i + 1) * blk, :] = w.astype(BF16)


def _fourier(p, w_f, layer, *, batch, seq_len):
    ch, sh, cc, sc, perm, alt = _dft_constants(seq_len)
    half = seq_len // 2
    full = lambda t: pl.BlockSpec(t.shape, lambda b: (0,) * t.ndim)
    return pl.pallas_call(
        _fourier_kernel,
        grid=(batch,),
        in_specs=[pl.BlockSpec((seq_len, FOURIER_W), lambda b: (b, 0)),
                  _layer_spec(w_f, layer, lambda b: (0, 0, 0)),
                  full(cc), full(sc), full(ch), full(sh), full(perm), full(alt)],
        out_specs=pl.BlockSpec((seq_len, FOURIER_W), lambda b: (b, 0)),
        out_shape=jax.ShapeDtypeStruct((batch * seq_len, FOURIER_W), BF16),
        scratch_shapes=[pltpu.VMEM((half, FOURIER_W), BF16), pltpu.VMEM((half, FOURIER_W), BF16)],
        compiler_params=_params("arbitrary"),
        name="fourier",
    )(p, w_f, cc, sc, ch, sh, perm, alt)


CONV_PAD = 16
CONV_ROWS = 128
CONV_LEAD = CONV_PAD - CONV_K // 2


def _conv_kernel(a_ref, gt_ref, wdw_ref, bdw_ref, gln_ref, bln_ref, wpw_ref, bpw_ref, o_ref, zp_scr):
    seq_len = a_ref.shape[0]
    zp_scr[0:CONV_PAD, :] = jnp.zeros((CONV_PAD, CONV_W), F32)
    zp_scr[CONV_PAD + seq_len:2 * CONV_PAD + seq_len, :] = jnp.zeros((CONV_PAD, CONV_W), F32)
    zp_scr[CONV_PAD:CONV_PAD + seq_len, :] = (a_ref[...].astype(F32)
                                              * jax.nn.sigmoid(gt_ref[...].astype(F32)))
    span = CONV_ROWS + 2 * CONV_PAD - SUBLANES

    def body(r, carry):
        base = pl.multiple_of(r * CONV_ROWS, CONV_ROWS)
        cols = []
        for c in range(CONV_W // LANES):
            win = zp_scr[pl.ds(base, CONV_ROWS + 2 * CONV_PAD), c * LANES:(c + 1) * LANES]
            acc = None
            for shift in range(SUBLANES):
                shifted = win if shift == 0 else pltpu.roll(win, win.shape[0] - shift, 0)
                for a in range(span // SUBLANES):
                    k = SUBLANES * a + shift - CONV_LEAD
                    if 0 <= k < CONV_K and SUBLANES * a + CONV_ROWS <= span:
                        term = (shifted[SUBLANES * a:SUBLANES * a + CONV_ROWS]
                                * wdw_ref[k:k + 1, c * LANES:(c + 1) * LANES])
                        acc = term if acc is None else acc + term
            cols.append(acc)
        z = jnp.concatenate(cols, axis=1) + bdw_ref[...]
        zc = z - jnp.mean(z, axis=-1, keepdims=True)
        y = zc * lax.rsqrt(jnp.mean(zc * zc, axis=-1, keepdims=True) + EPS) * gln_ref[...] + bln_ref[...]
        s = (y * jax.nn.sigmoid(y)).astype(BF16)
        out = jnp.dot(s, wpw_ref[...], preferred_element_type=F32) + bpw_ref[...]
        o_ref[pl.ds(base, CONV_ROWS), :] = out.astype(BF16)
        return carry

    lax.fori_loop(0, seq_len // CONV_ROWS, body, 0)


def _conv(p, w_dw, b_dw, g_ln, b_ln, w_pw, layer, b_pw, *, batch, seq_len):
    vec = pl.BlockSpec((1, CONV_W), lambda b: (0, 0))
    return pl.pallas_call(
        _conv_kernel,
        grid=(batch,),
        in_specs=[pl.BlockSpec((seq_len, CONV_W), lambda b: (b, 1)),
                  pl.BlockSpec((seq_len, CONV_W), lambda b: (b, 2)),
                  pl.BlockSpec(w_dw.shape, lambda b: (0, 0)),
                  vec, vec, vec,
                  _layer_spec(w_pw, layer, lambda b: (0, 0)),
                  vec],
        out_specs=pl.BlockSpec((seq_len, CONV_W), lambda b: (b, 0)),
        out_shape=jax.ShapeDtypeStruct((batch * seq_len, CONV_W), BF16),
        scratch_shapes=[pltpu.VMEM((seq_len + 2 * CONV_PAD, CONV_W), F32)],
        compiler_params=_params("arbitrary"),
        name="conv_module",
    )(p, p, w_dw, b_dw.reshape(1, -1), g_ln.reshape(1, -1), b_ln.reshape(1, -1), w_pw, b_pw.reshape(1, -1))


def _out_proj_kernel(yf_ref, yc_ref, ao_ref, w_ref, x_ref, gate_ref, gpost_ref, o_ref):
    y = jnp.dot(yf_ref[...], w_ref[0:FOURIER_W, :], preferred_element_type=F32)
    y = y + jnp.dot(yc_ref[...], w_ref[FOURIER_W:FOURIER_W + CONV_W, :], preferred_element_type=F32)
    y = y + jnp.dot(ao_ref[...], w_ref[FOURIER_W + CONV_W:, :], preferred_element_type=F32)
    o_ref[...] = x_ref[...] + gate_ref[...] * (_rms(y) * gpost_ref[...])


def _out_proj(yf, yc, ao, w, layer, x, modv, g_post, *, group_rows, tm):
    m, d = x.shape
    bpg = group_rows // tm
    return pl.pallas_call(
        _out_proj_kernel,
        grid=(m // tm,),
        in_specs=[pl.BlockSpec((tm, FOURIER_W), lambda i: (i, 0)),
                  pl.BlockSpec((tm, CONV_W), lambda i: (i, 0)),
                  pl.BlockSpec((tm, ATT_W), lambda i: (i, 0)),
                  _layer_spec(w, layer, lambda i: (0, 0), pipeline_mode=pl.Buffered(1)),
                  pl.BlockSpec((tm, d), lambda i: (i, 0)),
                  pl.BlockSpec((None, None, 1, d), lambda i: (i // bpg, 2, 0, 0)),
                  pl.BlockSpec((1, d), lambda i: (0, 0))],
        out_specs=pl.BlockSpec((tm, d), lambda i: (i, 0)),
        out_shape=jax.ShapeDtypeStruct((m, d), F32),
        compiler_params=_params("arbitrary"),
        name="out_proj",
    )(yf, yc, ao, w, x, modv, g_post.reshape(1, d))


def _mlp_kernel(x_ref, g_ref, sh_ref, sc_ref, gate_ref, gpost_ref, w1_ref, w2_ref, o_ref, h_scr):
    j = pl.program_id(1)
    last = pl.num_programs(1) - 1

    def ff_chunk(hb):
        h1 = jnp.maximum(jnp.dot(hb, w1_ref[...].astype(BF16), preferred_element_type=F32), 0.0)
        return jnp.dot((h1 * h1).astype(BF16), w2_ref[...].astype(BF16), preferred_element_type=F32)

    @pl.when(j == 0)
    def _():
        h = _rms(x_ref[...]) * g_ref[...]
        hb = (h * (1.0 + sc_ref[...]) + sh_ref[...]).astype(BF16)
        h_scr[...] = hb
        o_ref[...] = ff_chunk(hb)

    @pl.when(jnp.logical_and(j > 0, j < last))
    def _():
        o_ref[...] += ff_chunk(h_scr[...])

    @pl.when(j == last)
    def _():
        y = o_ref[...] + ff_chunk(h_scr[...])
        o_ref[...] = x_ref[...] + gate_ref[...] * (_rms(y) * gpost_ref[...])


def _mlp(x, g_pre, modv, g_post, w1, w2, layer, *, group_rows, tm, tf):
    m, d = x.shape
    dff = w1.shape[-1]
    bpg = group_rows // tm
    mod_spec = lambda k: pl.BlockSpec((None, None, 1, d), lambda i, j: (i // bpg, k, 0, 0))
    return pl.pallas_call(
        _mlp_kernel,
        grid=(m // tm, dff // tf),
        in_specs=[pl.BlockSpec((tm, d), lambda i, j: (i, 0)),
                  pl.BlockSpec((1, d), lambda i, j: (0, 0)),
                  mod_spec(3), mod_spec(4), mod_spec(5),
                  pl.BlockSpec((1, d), lambda i, j: (0, 0)),
                  _layer_spec(w1, layer, lambda i, j: (0, j), block_tail=(d, tf)),
                  _layer_spec(w2, layer, lambda i, j: (j, 0), block_tail=(tf, d))],
        out_specs=pl.BlockSpec((tm, d), lambda i, j: (i, 0)),
        out_shape=jax.ShapeDtypeStruct((m, d), F32),
        scratch_shapes=[pltpu.VMEM((tm, d), BF16)],
        compiler_params=pltpu.CompilerParams(dimension_semantics=("arbitrary", "arbitrary"),
                                             vmem_limit_bytes=V7X_VMEM_LIMIT_MLP_BYTES),
        name="mlp",
    )(x, g_pre.reshape(1, d), modv, modv, modv, g_post.reshape(1, d), w1, w2)


def _rope_tables(n):
    rows = n // GRID_W
    row = jnp.repeat(jnp.arange(rows), GRID_W).astype(F32)
    col = jnp.tile(jnp.arange(GRID_W), rows).astype(F32)
    n_freq = DIFF_HEAD_DIM // 4
    inv = ROPE_BASE ** (-jnp.arange(n_freq, dtype=F32) / n_freq)
    ang = jnp.concatenate([row[:, None] * inv, col[:, None] * inv], axis=-1)
    cos, sin = jnp.cos(ang), jnp.sin(ang)
    return (jnp.concatenate([cos, cos, cos, cos], axis=-1),
            jnp.concatenate([-sin, sin, -sin, sin], axis=-1))


def kernel(x, c, ctx, c_ctx, w_ada, b_ada, g_pre_mix, g_post_mix, g_pre_mlp, g_post_mlp, w_in, w_out, w_fourier, w_dw, b_dw, g_conv_ln, b_conv_ln, w_conv_pw, b_conv_pw, lambda_q1, lambda_k1, lambda_q2, lambda_k2, g_subln, w_mlp_in, w_mlp_out):
    bsz, n, d = x.shape
    n_ctx = ctx.shape[1]
    depth = w_ada.shape[0]

    cvec = jnp.zeros((8, d), F32).at[:bsz].set(c).at[bsz].set(c_ctx)
    mod = _adaln(cvec, w_ada, b_ada)

    w_in_b = w_in.astype(BF16)
    w_out_b = w_out.astype(BF16)
    w_pw_b = w_conv_pw.astype(BF16)
    cos, sin = _rope_tables(n)

    xl = x.reshape(bsz * n, d)
    xc = ctx.reshape(bsz * n_ctx, d)
    q_col = Q_CHUNKS[0] * CHUNK // LANES
    k_col = K_CHUNKS[0] * CHUNK // LANES
    v_col = KV_CHUNKS[2] * CHUNK // LANES

    for l in range(depth):
        last = l == depth - 1
        lam_init = 0.8 - 0.6 * math.exp(-0.3 * l)
        mod_lat = mod[l, :bsz].reshape(bsz, 6, 1, d)
        mod_ctx = mod[l, bsz:bsz + 1].reshape(1, 6, 1, d)
        lamp = jnp.zeros((8, LANES), F32)
        lamp = lamp.at[0, :DIFF_HEAD_DIM].set(lambda_q1[l]).at[1, :DIFF_HEAD_DIM].set(lambda_k1[l])
        lamp = lamp.at[2, :DIFF_HEAD_DIM].set(lambda_q2[l]).at[3, :DIFF_HEAD_DIM].set(lambda_k2[l])
        gsub = g_subln[l].reshape(1, LANES)

        p_lat = _in_proj(xl, g_pre_mix[l], mod_lat, w_in_b, l, cos, sin,
                         chunks=ALL_CHUNKS, group_rows=n, tm=512)
        ctx_chunks = KV_CHUNKS if last else ALL_CHUNKS
        p_ctx = _in_proj(xc, g_pre_mix[l], mod_ctx, w_in_b, l, None, None,
                         chunks=ctx_chunks, group_rows=bsz * n_ctx, tm=512)
        kc_col = ctx_chunks.index(K_CHUNKS[0]) * CHUNK // LANES
        vc_col = ctx_chunks.index(KV_CHUNKS[2]) * CHUNK // LANES

        ao = _attention(p_lat, q_col, [(p_ctx, kc_col, vc_col, n_ctx), (p_lat, k_col, v_col, n)],
                        lamp, gsub, batch=bsz, q_len=n, tq=n, lam_init=lam_init)
        yf = _fourier(p_lat, w_fourier, l, batch=bsz, seq_len=n)
        yc = _conv(p_lat, w_dw[l], b_dw[l], g_conv_ln[l], b_conv_ln[l], w_pw_b, l, b_conv_pw[l],
                   batch=bsz, seq_len=n)
        xl_mid = _out_proj(yf, yc, ao, w_out_b, l, xl, mod_lat, g_post_mix[l], group_rows=n, tm=512)

        if not last:
            aoc = _attention(p_ctx, q_col, [(p_ctx, kc_col, vc_col, n_ctx)], lamp, gsub,
                             batch=bsz, q_len=n_ctx, tq=n_ctx, lam_init=lam_init)
            yfc = _fourier(p_ctx, w_fourier, l, batch=bsz, seq_len=n_ctx)
            ycc = _conv(p_ctx, w_dw[l], b_dw[l], g_conv_ln[l], b_conv_ln[l], w_pw_b, l, b_conv_pw[l],
                        batch=bsz, seq_len=n_ctx)
            xc_mid = _out_proj(yfc, ycc, aoc, w_out_b, l, xc, mod_ctx, g_post_mix[l],
                               group_rows=bsz * n_ctx, tm=512)
            xc = _mlp(xc_mid, g_pre_mlp[l], mod_ctx, g_post_mlp[l], w_mlp_in, w_mlp_out, l,
                      group_rows=bsz * n_ctx, tm=1024, tf=512)

        xl = _mlp(xl_mid, g_pre_mlp[l], mod_lat, g_post_mlp[l], w_mlp_in, w_mlp_out, l,
                  group_rows=n, tm=1024, tf=512)

    return xl.reshape(bsz, n, d)
```

```python
import functools
import math

import numpy as np
import jax
import jax.numpy as jnp
from jax import lax
from jax.experimental import pallas as pl
from jax.experimental.pallas import tpu as pltpu

F32 = jnp.float32
BF16 = jnp.bfloat16

D_MODEL = 2048
GRID_W = 64
FOURIER_W = 512
FOURIER_GROUPS = 4
FOURIER_GW = 128
CONV_W = 512
CONV_K = 31
ATT_W = 1024
DIFF_HEAD_DIM = 64
DIFF_HEADS = 8
D_FF = 4 * D_MODEL
ROPE_BASE = 10000.0
EPS = 1e-6
IN_COLS = 4608

CHUNK = 512
Q_CHUNKS = (3, 4)
K_CHUNKS = (5, 6)
ALL_CHUNKS = tuple(range(IN_COLS // CHUNK))
KV_CHUNKS = (5, 6, 7, 8)

Q_SCALE = DIFF_HEAD_DIM ** -0.5 * math.log2(math.e)

LANES = 128
SUBLANES = 8
V7X_VMEM_LIMIT_BYTES = 56 * 1024 * 1024
V7X_VMEM_LIMIT_MLP_BYTES = 62 * 1024 * 1024


def _params(*semantics):
    return pltpu.CompilerParams(dimension_semantics=semantics,
                                vmem_limit_bytes=V7X_VMEM_LIMIT_BYTES)


def _rms(y):
    return y * lax.rsqrt(jnp.mean(y * y, axis=-1, keepdims=True) + EPS)


def _layer_spec(w, layer, index_map_tail, block_tail=None, **kw):
    block_tail = w.shape[1:] if block_tail is None else block_tail
    return pl.BlockSpec((None,) + tuple(block_tail), lambda *g: (layer,) + tuple(index_map_tail(*g)), **kw)


def _adaln_kernel(c_ref, w_ref, b_ref, o_ref):
    s = jax.nn.silu(c_ref[...]).astype(BF16)
    o_ref[...] = jnp.dot(s, w_ref[...].astype(BF16), preferred_element_type=F32) + b_ref[...]


def _adaln(cvec, w_ada, b_ada):
    depth, d, n6 = w_ada.shape
    tn = 1024
    return pl.pallas_call(
        _adaln_kernel,
        grid=(depth, n6 // tn),
        in_specs=[pl.BlockSpec((8, d), lambda l, j: (0, 0)),
                  pl.BlockSpec((None, d, tn), lambda l, j: (l, 0, j)),
                  pl.BlockSpec((None, 1, tn), lambda l, j: (l, 0, j))],
        out_specs=pl.BlockSpec((None, 8, tn), lambda l, j: (l, 0, j)),
        out_shape=jax.ShapeDtypeStruct((depth, 8, n6), F32),
        compiler_params=_params("arbitrary", "arbitrary"),
        name="adaln",
    )(cvec, w_ada, b_ada.reshape(depth, 1, n6))


def _in_proj_kernel(*refs, chunks, rope):
    if rope:
        x_ref, g_ref, sh_ref, sc_ref, w_ref, cos_ref, sin_ref, o_ref = refs
    else:
        x_ref, g_ref, sh_ref, sc_ref, w_ref, o_ref = refs
    h = _rms(x_ref[...]) * g_ref[...]
    hb = (h * (1.0 + sc_ref[...]) + sh_ref[...]).astype(BF16)
    if rope:
        cos = cos_ref[...]
        sin = sin_ref[...]
        lane = lax.broadcasted_iota(jnp.int32, cos.shape, 1)
        first_half = (lane % DIFF_HEAD_DIM) < (DIFF_HEAD_DIM // 2)
    for n, j in enumerate(chunks):
        acc = jnp.dot(hb, w_ref[:, j * CHUNK:(j + 1) * CHUNK], preferred_element_type=F32)
        if j in Q_CHUNKS or j in K_CHUNKS:
            for s in range(CHUNK // LANES):
                t = acc[:, s * LANES:(s + 1) * LANES]
                if rope:
                    swapped = jnp.where(first_half,
                                        pltpu.roll(t, LANES - DIFF_HEAD_DIM // 2, 1),
                                        pltpu.roll(t, DIFF_HEAD_DIM // 2, 1))
                    t = t * cos + swapped * sin
                if j in Q_CHUNKS:
                    t = t * Q_SCALE
                o_ref[:, n * CHUNK + s * LANES:n * CHUNK + (s + 1) * LANES] = t.astype(BF16)
        else:
            o_ref[:, n * CHUNK:(n + 1) * CHUNK] = acc.astype(BF16)


def _in_proj(x, g_pre, modv, w, layer, cos, sin, *, chunks, group_rows, tm):
    m, d = x.shape
    bpg = group_rows // tm
    rope = cos is not None
    in_specs = [pl.BlockSpec((tm, d), lambda i: (i, 0)),
                pl.BlockSpec((1, d), lambda i: (0, 0)),
                pl.BlockSpec((None, None, 1, d), lambda i: (i // bpg, 0, 0, 0)),
                pl.BlockSpec((None, None, 1, d), lambda i: (i // bpg, 1, 0, 0)),
                _layer_spec(w, layer, lambda i: (0, 0), pipeline_mode=pl.Buffered(1))]
    args = [x, g_pre.reshape(1, d), modv, modv, w]
    if rope:
        in_specs += [pl.BlockSpec((tm, LANES), lambda i: (i % bpg, 0)),
                     pl.BlockSpec((tm, LANES), lambda i: (i % bpg, 0))]
        args += [cos, sin]
    ncol = len(chunks) * CHUNK
    return pl.pallas_call(
        functools.partial(_in_proj_kernel, chunks=chunks, rope=rope),
        grid=(m // tm,),
        in_specs=in_specs,
        out_specs=pl.BlockSpec((tm, ncol), lambda i: (i, 0)),
        out_shape=jax.ShapeDtypeStruct((m, ncol), BF16),
        compiler_params=_params("arbitrary"),
        name="in_proj",
    )(*args)


ATT_UNIT_ROWS = 512


def _attn_kernel(*refs, kv_lens, heads, lam_init):
    n_kv = len(kv_lens)
    q_ref = refs[0]
    kv_refs = refs[1:1 + 2 * n_kv]
    lamp_ref, gsub_ref, o_ref, kt_scr, v_scr = refs[1 + 2 * n_kv:]
    head_cols = [slice(h * LANES, (h + 1) * LANES) for h in range(heads)]

    @pl.when(pl.program_id(2) == 0)
    def _():
        for h, cols in enumerate(head_cols):
            off = 0
            for i, n in enumerate(kv_lens):
                kt_scr[h, :, off:off + n] = kv_refs[2 * i][:, cols].T
                v_scr[h, off:off + n, 0:LANES] = kv_refs[2 * i + 1][:, cols]
                off += n
            v_scr[h, :, LANES:2 * LANES] = jnp.ones((off, LANES), BF16)

    lp = lamp_ref[...]
    lam = (jnp.exp(jnp.sum(lp[0:1] * lp[1:2], axis=-1, keepdims=True))
           - jnp.exp(jnp.sum(lp[2:3] * lp[3:4], axis=-1, keepdims=True)) + lam_init)
    rows = min(ATT_UNIT_ROWS, q_ref.shape[0])
    units = [(h, u) for h in range(heads) for u in range(q_ref.shape[0] // rows)]

    def scores(unit):
        h, u = unit
        q = q_ref[u * rows:(u + 1) * rows, head_cols[h]]
        lane = lax.broadcasted_iota(jnp.int32, q.shape, 1)
        zero = jnp.zeros_like(q)
        qq = jnp.concatenate([jnp.where(lane < DIFF_HEAD_DIM, q, zero),
                              jnp.where(lane < DIFF_HEAD_DIM, zero, q)], axis=0)
        return jnp.dot(qq, kt_scr[h], preferred_element_type=F32)

    s_next = scores(units[0])
    for n, (h, u) in enumerate(units):
        s = s_next
        if n + 1 < len(units):
            s_next = scores(units[n + 1])
        e = jnp.exp2(s - s.max(axis=-1, keepdims=True)).astype(BF16)
        acc1 = jnp.dot(e[:rows], v_scr[h], preferred_element_type=F32)
        acc2 = jnp.dot(e[rows:], v_scr[h], preferred_element_type=F32)
        o = (acc1[:, 0:LANES] / acc1[:, LANES:2 * LANES]
             - lam * (acc2[:, 0:LANES] / acc2[:, LANES:2 * LANES]))
        o_ref[u * rows:(u + 1) * rows, head_cols[h]] = (
            _rms(o) * gsub_ref[...] * (1.0 - lam_init)).astype(BF16)


def _attention(q_arr, q_col, kvs, lamp, gsub, *, batch, q_len, tq, heads, lam_init):
    nqb = q_len // tq
    width = heads * LANES
    assert all(c % heads == 0 for c in [q_col] + [kv[1] for kv in kvs] + [kv[2] for kv in kvs])
    in_specs = [pl.BlockSpec((tq, width), lambda b, h, i: (b * nqb + i, q_col // heads + h))]
    args = [q_arr]
    for arr, k_col, v_col, kv_len in kvs:
        in_specs.append(pl.BlockSpec((kv_len, width), lambda b, h, i, c=k_col // heads: (b, c + h)))
        in_specs.append(pl.BlockSpec((kv_len, width), lambda b, h, i, c=v_col // heads: (b, c + h)))
        args += [arr, arr]
    in_specs += [pl.BlockSpec(lamp.shape, lambda b, h, i: (0, 0)),
                 pl.BlockSpec(gsub.shape, lambda b, h, i: (0, 0))]
    args += [lamp, gsub]
    kv_lens = tuple(kv[3] for kv in kvs)
    return pl.pallas_call(
        functools.partial(_attn_kernel, kv_lens=kv_lens, heads=heads, lam_init=lam_init),
        grid=(batch, DIFF_HEADS // heads, nqb),
        in_specs=in_specs,
        out_specs=pl.BlockSpec((tq, width), lambda b, h, i: (b * nqb + i, h)),
        out_shape=jax.ShapeDtypeStruct((batch * q_len, ATT_W), BF16),
        scratch_shapes=[pltpu.VMEM((heads, LANES, sum(kv_lens)), BF16),
                        pltpu.VMEM((heads, sum(kv_lens), 2 * LANES), BF16)],
        compiler_params=_params("arbitrary", "arbitrary", "arbitrary"),
        name="diff_attn",
    )(*args)


FOURIER_BLK = 128


def _dft_constants(seq_len):
    half = seq_len // 2
    k = np.arange(half, dtype=np.int64)
    ang = 2.0 * np.pi * ((k[:, None] * k[None, :]) % seq_len).astype(np.float64) / seq_len
    ch = np.cos(ang).astype(np.float32)
    sh = np.sin(ang).astype(np.float32)
    c = np.arange(FOURIER_GW, dtype=np.int64)
    angc = 2.0 * np.pi * ((c[:, None] * c[None, :]) % FOURIER_GW).astype(np.float64) / FOURIER_GW
    norm = 1.0 / math.sqrt(seq_len * FOURIER_GW)
    cc = (np.cos(angc) * norm).astype(np.float32)
    sc = (np.sin(angc) * norm).astype(np.float32)
    perm = np.zeros((FOURIER_BLK, 2 * FOURIER_BLK), np.float32)
    r = np.arange(1, FOURIER_BLK)
    perm[r, FOURIER_BLK - r] = 1.0
    perm[0, FOURIER_BLK] = 1.0
    alt = np.zeros((SUBLANES, half), np.float32)
    alt[0] = 1.0 - 2.0 * (k % 2)
    return tuple(jnp.asarray(t).astype(BF16) for t in (ch, sh, cc, sc, perm, alt))


def _fourier_kernel(u_ref, wf_ref, cc_ref, sc_ref, ch_ref, sh_ref, perm_ref, alt_ref, o_ref, r1_scr, r2_scr):
    seq_len = u_ref.shape[0]
    half = seq_len // 2
    blk = FOURIER_BLK
    nb = half // blk
    groups = [slice(g * FOURIER_GW, (g + 1) * FOURIER_GW) for g in range(FOURIER_GROUPS)]
    perm = perm_ref[...]
    rev = perm[:, 0:blk]
    mix_a, mix_b = [], []
    for g in groups:
        wf = wf_ref[g.start // FOURIER_GW].astype(BF16)
        mix_a.append(jnp.dot(cc_ref[...], wf, preferred_element_type=F32).astype(BF16))
        mix_b.append(jnp.dot(sc_ref[...], wf, preferred_element_type=F32).astype(BF16))

    for i in range(nb):
        if i == 0:
            u_rev = jnp.dot(rev, u_ref[seq_len - blk:seq_len, :], preferred_element_type=F32)
        else:
            u_rev = jnp.dot(perm, u_ref[seq_len - (i + 1) * blk:seq_len - (i - 1) * blk, :],
                            preferred_element_type=F32)
        u_blk = u_ref[i * blk:(i + 1) * blk, :].astype(F32)
        ue = (u_blk + u_rev).astype(BF16)
        uo = (u_blk - u_rev).astype(BF16)
        for g, a, b in zip(groups, mix_a, mix_b):
            r1_scr[i * blk:(i + 1) * blk, g] = jnp.dot(ue[:, g], a, preferred_element_type=F32).astype(BF16)
            r2_scr[i * blk:(i + 1) * blk, g] = jnp.dot(uo[:, g], b, preferred_element_type=F32).astype(BF16)

    u_mid = u_ref[half:half + SUBLANES, :]
    t_mid = jnp.concatenate([jnp.dot(u_mid[:, g], a, preferred_element_type=F32)
                             for g, a in zip(groups, mix_a)], axis=1)[0:1]
    k = lax.broadcasted_iota(jnp.int32, (half, 1), 0)
    sign = jnp.where(k % 2 == 0, 1.0, -1.0).astype(F32)
    p = jnp.dot(ch_ref[...], r1_scr[...], preferred_element_type=F32) + sign * t_mid
    q = jnp.dot(sh_ref[...], r2_scr[...], preferred_element_type=F32)
    o_ref[0:half, :] = (p - q).astype(BF16)
    z = (p + q).astype(BF16)
    y_mid = jnp.dot(alt_ref[...], r1_scr[...], preferred_element_type=F32)[0:1] + t_mid
    row = lax.broadcasted_iota(jnp.int32, (blk, 1), 0)
    for i in range(nb):
        if i == 0:
            w = jnp.dot(rev, z[half - blk:half], preferred_element_type=F32)
            w = jnp.where(row == 0, y_mid, w)
        else:
            w = jnp.dot(perm, z[half - (i + 1) * blk:half - (i - 1) * blk], preferred_element_type=F32)
        o_ref[half + i * blk:half + (i + 1) * blk, :] = w.astype(BF16)


def _fourier(p, w_f, layer, *, batch, seq_len):
    ch, sh, cc, sc, perm, alt = _dft_constants(seq_len)
    half = seq_len // 2
    full = lambda t: pl.BlockSpec(t.shape, lambda b: (0,) * t.ndim)
    return pl.pallas_call(
        _fourier_kernel,
        grid=(batch,),
        in_specs=[pl.BlockSpec((seq_len, FOURIER_W), lambda b: (b, 0)),
                  _layer_spec(w_f, layer, lambda b: (0, 0, 0)),
                  full(cc), full(sc), full(ch), full(sh), full(perm), full(alt)],
        out_specs=pl.BlockSpec((seq_len, FOURIER_W), lambda b: (b, 0)),
        out_shape=jax.ShapeDtypeStruct((batch * seq_len, FOURIER_W), BF16),
        scratch_shapes=[pltpu.VMEM((half, FOURIER_W), BF16), pltpu.VMEM((half, FOURIER_W), BF16)],
        compiler_params=_params("arbitrary"),
        name="fourier",
    )(p, w_f, cc, sc, ch, sh, perm, alt)


CONV_PAD = 16
CONV_ROWS = 128
CONV_LEAD = CONV_PAD - CONV_K // 2


def _conv_kernel(a_ref, gt_ref, wdw_ref, bdw_ref, gln_ref, bln_ref, wpw_ref, bpw_ref, o_ref, zp_scr):
    seq_len = a_ref.shape[0]
    zp_scr[0:CONV_PAD, :] = jnp.zeros((CONV_PAD, CONV_W), F32)
    zp_scr[CONV_PAD + seq_len:2 * CONV_PAD + seq_len, :] = jnp.zeros((CONV_PAD, CONV_W), F32)
    zp_scr[CONV_PAD:CONV_PAD + seq_len, :] = (a_ref[...].astype(F32)
                                              * jax.nn.sigmoid(gt_ref[...].astype(F32)))
    span = CONV_ROWS + 2 * CONV_PAD - SUBLANES

    def body(r, carry):
        base = pl.multiple_of(r * CONV_ROWS, CONV_ROWS)
        cols = []
        for c in range(CONV_W // LANES):
            win = zp_scr[pl.ds(base, CONV_ROWS + 2 * CONV_PAD), c * LANES:(c + 1) * LANES]
            acc = None
            for shift in range(SUBLANES):
                shifted = win if shift == 0 else pltpu.roll(win, win.shape[0] - shift, 0)
                for a in range(span // SUBLANES):
                    k = SUBLANES * a + shift - CONV_LEAD
                    if 0 <= k < CONV_K and SUBLANES * a + CONV_ROWS <= span:
                        term = (shifted[SUBLANES * a:SUBLANES * a + CONV_ROWS]
                                * wdw_ref[k:k + 1, c * LANES:(c + 1) * LANES])
                        acc = term if acc is None else acc + term
            cols.append(acc)
        z = jnp.concatenate(cols, axis=1) + bdw_ref[...]
        zc = z - jnp.mean(z, axis=-1, keepdims=True)
        y = zc * lax.rsqrt(jnp.mean(zc * zc, axis=-1, keepdims=True) + EPS) * gln_ref[...] + bln_ref[...]
        s = (y * jax.nn.sigmoid(y)).astype(BF16)
        out = jnp.dot(s, wpw_ref[...], preferred_element_type=F32) + bpw_ref[...]
        o_ref[pl.ds(base, CONV_ROWS), :] = out.astype(BF16)
        return carry

    lax.fori_loop(0, seq_len // CONV_ROWS, body, 0)


def _conv(p, w_dw, b_dw, g_ln, b_ln, w_pw, layer, b_pw, *, batch, seq_len):
    vec = pl.BlockSpec((1, CONV_W), lambda b: (0, 0))
    return pl.pallas_call(
        _conv_kernel,
        grid=(batch,),
        in_specs=[pl.BlockSpec((seq_len, CONV_W), lambda b: (b, 1)),
                  pl.BlockSpec((seq_len, CONV_W), lambda b: (b, 2)),
                  pl.BlockSpec(w_dw.shape, lambda b: (0, 0)),
                  vec, vec, vec,
                  _layer_spec(w_pw, layer, lambda b: (0, 0)),
                  vec],
        out_specs=pl.BlockSpec((seq_len, CONV_W), lambda b: (b, 0)),
        out_shape=jax.ShapeDtypeStruct((batch * seq_len, CONV_W), BF16),
        scratch_shapes=[pltpu.VMEM((seq_len + 2 * CONV_PAD, CONV_W), F32)],
        compiler_params=_params("arbitrary"),
        name="conv_module",
    )(p, p, w_dw, b_dw.reshape(1, -1), g_ln.reshape(1, -1), b_ln.reshape(1, -1), w_pw, b_pw.reshape(1, -1))


def _out_proj_kernel(yf_ref, yc_ref, ao_ref, w_ref, x_ref, gate_ref, gpost_ref, o_ref, w_scr):
    @pl.when(pl.program_id(0) == 0)
    def _():
        w_scr[...] = w_ref[...].astype(BF16)

    y = jnp.dot(yf_ref[...], w_scr[0:FOURIER_W, :], preferred_element_type=F32)
    y = y + jnp.dot(yc_ref[...], w_scr[FOURIER_W:FOURIER_W + CONV_W, :], preferred_element_type=F32)
    y = y + jnp.dot(ao_ref[...], w_scr[FOURIER_W + CONV_W:, :], preferred_element_type=F32)
    o_ref[...] = x_ref[...] + gate_ref[...] * (_rms(y) * gpost_ref[...])


def _out_proj(yf, yc, ao, w, layer, x, modv, g_post, *, group_rows, tm):
    m, d = x.shape
    bpg = group_rows // tm
    return pl.pallas_call(
        _out_proj_kernel,
        grid=(m // tm,),
        in_specs=[pl.BlockSpec((tm, FOURIER_W), lambda i: (i, 0)),
                  pl.BlockSpec((tm, CONV_W), lambda i: (i, 0)),
                  pl.BlockSpec((tm, ATT_W), lambda i: (i, 0)),
                  _layer_spec(w, layer, lambda i: (0, 0), pipeline_mode=pl.Buffered(1)),
                  pl.BlockSpec((tm, d), lambda i: (i, 0)),
                  pl.BlockSpec((None, None, 1, d), lambda i: (i // bpg, 2, 0, 0)),
                  pl.BlockSpec((1, d), lambda i: (0, 0))],
        out_specs=pl.BlockSpec((tm, d), lambda i: (i, 0)),
        out_shape=jax.ShapeDtypeStruct((m, d), F32),
        scratch_shapes=[pltpu.VMEM(w.shape[1:], BF16)],
        compiler_params=_params("arbitrary"),
        name="out_proj",
    )(yf, yc, ao, w, x, modv, g_post.reshape(1, d))


def _mlp_kernel(x_ref, g_ref, sh_ref, sc_ref, gate_ref, gpost_ref, w1_ref, w2_ref, o_ref, h_scr):
    j = pl.program_id(1)
    last = pl.num_programs(1) - 1

    def ff_chunk(hb):
        h1 = jnp.maximum(jnp.dot(hb, w1_ref[...].astype(BF16), preferred_element_type=F32), 0.0)
        return jnp.dot((h1 * h1).astype(BF16), w2_ref[...].astype(BF16), preferred_element_type=F32)

    @pl.when(j == 0)
    def _():
        h = _rms(x_ref[...]) * g_ref[...]
        hb = (h * (1.0 + sc_ref[...]) + sh_ref[...]).astype(BF16)
        h_scr[...] = hb
        o_ref[...] = ff_chunk(hb)

    @pl.when(jnp.logical_and(j > 0, j < last))
    def _():
        o_ref[...] += ff_chunk(h_scr[...])

    @pl.when(j == last)
    def _():
        y = o_ref[...] + ff_chunk(h_scr[...])
        o_ref[...] = x_ref[...] + gate_ref[...] * (_rms(y) * gpost_ref[...])


def _mlp(x, g_pre, modv, g_post, w1, w2, layer, *, group_rows, tm, tf):
    m, d = x.shape
    dff = w1.shape[-1]
    bpg = group_rows // tm
    mod_spec = lambda k: pl.BlockSpec((None, None, 1, d), lambda i, j: (i // bpg, k, 0, 0))
    return pl.pallas_call(
        _mlp_kernel,
        grid=(m // tm, dff // tf),
        in_specs=[pl.BlockSpec((tm, d), lambda i, j: (i, 0)),
                  pl.BlockSpec((1, d), lambda i, j: (0, 0)),
                  mod_spec(3), mod_spec(4), mod_spec(5),
                  pl.BlockSpec((1, d), lambda i, j: (0, 0)),
                  _layer_spec(w1, layer, lambda i, j: (0, j), block_tail=(d, tf)),
                  _layer_spec(w2, layer, lambda i, j: (j, 0), block_tail=(tf, d))],
        out_specs=pl.BlockSpec((tm, d), lambda i, j: (i, 0)),
        out_shape=jax.ShapeDtypeStruct((m, d), F32),
        scratch_shapes=[pltpu.VMEM((tm, d), BF16)],
        compiler_params=pltpu.CompilerParams(dimension_semantics=("arbitrary", "arbitrary"),
                                             vmem_limit_bytes=V7X_VMEM_LIMIT_MLP_BYTES),
        name="mlp",
    )(x, g_pre.reshape(1, d), modv, modv, modv, g_post.reshape(1, d), w1, w2)


def _rope_tables(n):
    rows = n // GRID_W
    row = jnp.repeat(jnp.arange(rows), GRID_W).astype(F32)
    col = jnp.tile(jnp.arange(GRID_W), rows).astype(F32)
    n_freq = DIFF_HEAD_DIM // 4
    inv = ROPE_BASE ** (-jnp.arange(n_freq, dtype=F32) / n_freq)
    ang = jnp.concatenate([row[:, None] * inv, col[:, None] * inv], axis=-1)
    cos, sin = jnp.cos(ang), jnp.sin(ang)
    return (jnp.concatenate([cos, cos, cos, cos], axis=-1),
            jnp.concatenate([-sin, sin, -sin, sin], axis=-1))


def kernel(x, c, ctx, c_ctx, w_ada, b_ada, g_pre_mix, g_post_mix, g_pre_mlp, g_post_mlp, w_in, w_out, w_fourier, w_dw, b_dw, g_conv_ln, b_conv_ln, w_conv_pw, b_conv_pw, lambda_q1, lambda_k1, lambda_q2, lambda_k2, g_subln, w_mlp_in, w_mlp_out):
    bsz, n, d = x.shape
    n_ctx = ctx.shape[1]
    depth = w_ada.shape[0]

    cvec = jnp.zeros((8, d), F32).at[:bsz].set(c).at[bsz].set(c_ctx)
    mod = _adaln(cvec, w_ada, b_ada)

    w_in_b = w_in.astype(BF16)
    w_pw_b = w_conv_pw.astype(BF16)
    cos, sin = _rope_tables(n)

    xl = x.reshape(bsz * n, d)
    xc = ctx.reshape(bsz * n_ctx, d)
    q_col = Q_CHUNKS[0] * CHUNK // LANES
    k_col = K_CHUNKS[0] * CHUNK // LANES
    v_col = KV_CHUNKS[2] * CHUNK // LANES

    for l in range(depth):
        last = l == depth - 1
        lam_init = 0.8 - 0.6 * math.exp(-0.3 * l)
        mod_lat = mod[l, :bsz].reshape(bsz, 6, 1, d)
        mod_ctx = mod[l, bsz:bsz + 1].reshape(1, 6, 1, d)
        lamp = jnp.zeros((8, LANES), F32)
        lamp = lamp.at[0, :DIFF_HEAD_DIM].set(lambda_q1[l]).at[1, :DIFF_HEAD_DIM].set(lambda_k1[l])
        lamp = lamp.at[2, :DIFF_HEAD_DIM].set(lambda_q2[l]).at[3, :DIFF_HEAD_DIM].set(lambda_k2[l])
        gsub = g_subln[l].reshape(1, LANES)

        p_lat = _in_proj(xl, g_pre_mix[l], mod_lat, w_in_b, l, cos, sin,
                         chunks=ALL_CHUNKS, group_rows=n, tm=512)
        ctx_chunks = KV_CHUNKS if last else ALL_CHUNKS
        p_ctx = _in_proj(xc, g_pre_mix[l], mod_ctx, w_in_b, l, None, None,
                         chunks=ctx_chunks, group_rows=bsz * n_ctx, tm=512)
        kc_col = ctx_chunks.index(K_CHUNKS[0]) * CHUNK // LANES
        vc_col = ctx_chunks.index(KV_CHUNKS[2]) * CHUNK // LANES

        ao = _attention(p_lat, q_col, [(p_ctx, kc_col, vc_col, n_ctx), (p_lat, k_col, v_col, n)],
                        lamp, gsub, batch=bsz, q_len=n, tq=n, heads=1, lam_init=lam_init)
        yf = _fourier(p_lat, w_fourier, l, batch=bsz, seq_len=n)
        yc = _conv(p_lat, w_dw[l], b_dw[l], g_conv_ln[l], b_conv_ln[l], w_pw_b, l, b_conv_pw[l],
                   batch=bsz, seq_len=n)
        xl_mid = _out_proj(yf, yc, ao, w_out, l, xl, mod_lat, g_post_mix[l], group_rows=n, tm=512)

        if not last:
            aoc = _attention(p_ctx, q_col, [(p_ctx, kc_col, vc_col, n_ctx)], lamp, gsub,
                             batch=bsz, q_len=n_ctx, tq=n_ctx, heads=4, lam_init=lam_init)
            yfc = _fourier(p_ctx, w_fourier, l, batch=bsz, seq_len=n_ctx)
            ycc = _conv(p_ctx, w_dw[l], b_dw[l], g_conv_ln[l], b_conv_ln[l], w_pw_b, l, b_conv_pw[l],
                        batch=bsz, seq_len=n_ctx)
            xc_mid = _out_proj(yfc, ycc, aoc, w_out, l, xc, mod_ctx, g_post_mix[l],
                               group_rows=bsz * n_ctx, tm=512)
            xc = _mlp(xc_mid, g_pre_mlp[l], mod_ctx, g_post_mlp[l], w_mlp_in, w_mlp_out, l,
                      group_rows=bsz * n_ctx, tm=1024, tf=512)

        xl = _mlp(xl_mid, g_pre_mlp[l], mod_lat, g_post_mlp[l], w_mlp_in, w_mlp_out, l,
                  group_rows=n, tm=1024, tf=512)

    return xl.reshape(bsz, n, d)
```

```python
import functools
import math

import numpy as np
import jax
import jax.numpy as jnp
from jax import lax
from jax.experimental import pallas as pl
from jax.experimental.pallas import tpu as pltpu

F32 = jnp.float32
BF16 = jnp.bfloat16

D_MODEL = 2048
GRID_W = 64
FOURIER_W = 512
FOURIER_GROUPS = 4
FOURIER_GW = 128
CONV_W = 512
CONV_K = 31
ATT_W = 1024
DIFF_HEAD_DIM = 64
DIFF_HEADS = 8
D_FF = 4 * D_MODEL
ROPE_BASE = 10000.0
EPS = 1e-6
IN_COLS = 4608

CHUNK = 512
Q_CHUNKS = (3, 4)
K_CHUNKS = (5, 6)
ALL_CHUNKS = tuple(range(IN_COLS // CHUNK))
KV_CHUNKS = (5, 6, 7, 8)

Q_SCALE = DIFF_HEAD_DIM ** -0.5 * math.log2(math.e)

LANES = 128
SUBLANES = 8
V7X_VMEM_LIMIT_BYTES = 56 * 1024 * 1024
V7X_VMEM_LIMIT_MLP_BYTES = 62 * 1024 * 1024


def _params(*semantics):
    return pltpu.CompilerParams(dimension_semantics=semantics,
                                vmem_limit_bytes=V7X_VMEM_LIMIT_BYTES)


def _rms(y):
    return y * lax.rsqrt(jnp.mean(y * y, axis=-1, keepdims=True) + EPS)


def _layer_spec(w, layer, index_map_tail, block_tail=None, **kw):
    block_tail = w.shape[1:] if block_tail is None else block_tail
    return pl.BlockSpec((None,) + tuple(block_tail), lambda *g: (layer,) + tuple(index_map_tail(*g)), **kw)


def _adaln_kernel(c_ref, w_ref, b_ref, o_ref):
    s = jax.nn.silu(c_ref[...]).astype(BF16)
    o_ref[...] = jnp.dot(s, w_ref[...].astype(BF16), preferred_element_type=F32) + b_ref[...]


def _adaln(cvec, w_ada, b_ada):
    depth, d, n6 = w_ada.shape
    tn = 1024
    return pl.pallas_call(
        _adaln_kernel,
        grid=(depth, n6 // tn),
        in_specs=[pl.BlockSpec((8, d), lambda l, j: (0, 0)),
                  pl.BlockSpec((None, d, tn), lambda l, j: (l, 0, j)),
                  pl.BlockSpec((None, 1, tn), lambda l, j: (l, 0, j))],
        out_specs=pl.BlockSpec((None, 8, tn), lambda l, j: (l, 0, j)),
        out_shape=jax.ShapeDtypeStruct((depth, 8, n6), F32),
        compiler_params=_params("arbitrary", "arbitrary"),
        name="adaln",
    )(cvec, w_ada, b_ada.reshape(depth, 1, n6))


def _in_proj_kernel(*refs, chunks, rope, layer):
    if rope:
        x_ref, g_ref, sh_ref, sc_ref, w_hbm, cos_ref, sin_ref, o_ref, w_scr, stage, sem = refs
    else:
        x_ref, g_ref, sh_ref, sc_ref, w_hbm, o_ref, w_scr, stage, sem = refs

    def chunk_copy(n):
        slot = n % 2
        return pltpu.make_async_copy(w_hbm.at[layer, :, pl.ds(chunks[n] * CHUNK, CHUNK)],
                                     stage.at[slot], sem.at[slot])

    def normed():
        h = _rms(x_ref[...]) * g_ref[...]
        return (h * (1.0 + sc_ref[...]) + sh_ref[...]).astype(BF16)

    def project(hb, n):
        j = chunks[n]
        acc = jnp.dot(hb, w_scr[:, n * CHUNK:(n + 1) * CHUNK], preferred_element_type=F32)
        if j in Q_CHUNKS or j in K_CHUNKS:
            if rope:
                cos = cos_ref[...]
                sin = sin_ref[...]
                lane = lax.broadcasted_iota(jnp.int32, cos.shape, 1)
                first_half = (lane % DIFF_HEAD_DIM) < (DIFF_HEAD_DIM // 2)
            for s in range(CHUNK // LANES):
                t = acc[:, s * LANES:(s + 1) * LANES]
                if rope:
                    swapped = jnp.where(first_half,
                                        pltpu.roll(t, LANES - DIFF_HEAD_DIM // 2, 1),
                                        pltpu.roll(t, DIFF_HEAD_DIM // 2, 1))
                    t = t * cos + swapped * sin
                if j in Q_CHUNKS:
                    t = t * Q_SCALE
                o_ref[:, n * CHUNK + s * LANES:n * CHUNK + (s + 1) * LANES] = t.astype(BF16)
        else:
            o_ref[:, n * CHUNK:(n + 1) * CHUNK] = acc.astype(BF16)

    @pl.when(pl.program_id(0) == 0)
    def _():
        chunk_copy(0).start()
        hb = normed()
        for n in range(len(chunks)):
            if n + 1 < len(chunks):
                chunk_copy(n + 1).start()
            chunk_copy(n).wait()
            w_scr[:, n * CHUNK:(n + 1) * CHUNK] = stage[n % 2].astype(BF16)
            project(hb, n)

    @pl.when(pl.program_id(0) != 0)
    def _():
        hb = normed()
        for n in range(len(chunks)):
            project(hb, n)


def _in_proj(x, g_pre, modv, w, layer, cos, sin, *, chunks, group_rows, tm):
    m, d = x.shape
    bpg = group_rows // tm
    rope = cos is not None
    in_specs = [pl.BlockSpec((tm, d), lambda i: (i, 0)),
                pl.BlockSpec((1, d), lambda i: (0, 0)),
                pl.BlockSpec((None, None, 1, d), lambda i: (i // bpg, 0, 0, 0)),
                pl.BlockSpec((None, None, 1, d), lambda i: (i // bpg, 1, 0, 0)),
                pl.BlockSpec(memory_space=pl.ANY)]
    args = [x, g_pre.reshape(1, d), modv, modv, w]
    if rope:
        in_specs += [pl.BlockSpec((tm, LANES), lambda i: (i % bpg, 0)),
                     pl.BlockSpec((tm, LANES), lambda i: (i % bpg, 0))]
        args += [cos, sin]
    ncol = len(chunks) * CHUNK
    return pl.pallas_call(
        functools.partial(_in_proj_kernel, chunks=chunks, rope=rope, layer=layer),
        grid=(m // tm,),
        in_specs=in_specs,
        out_specs=pl.BlockSpec((tm, ncol), lambda i: (i, 0)),
        out_shape=jax.ShapeDtypeStruct((m, ncol), BF16),
        scratch_shapes=[pltpu.VMEM((d, ncol), BF16),
                        pltpu.VMEM((2, d, CHUNK), F32),
                        pltpu.SemaphoreType.DMA((2,))],
        compiler_params=_params("arbitrary"),
        name="in_proj",
    )(*args)


ATT_UNIT_ROWS = 512


def _attn_kernel(*refs, kv_lens, heads, lam_init):
    n_kv = len(kv_lens)
    q_ref = refs[0]
    kv_refs = refs[1:1 + 2 * n_kv]
    lamp_ref, gsub_ref, o_ref, kt_scr, v_scr = refs[1 + 2 * n_kv:]
    head_cols = [slice(h * LANES, (h + 1) * LANES) for h in range(heads)]

    @pl.when(pl.program_id(2) == 0)
    def _():
        for h, cols in enumerate(head_cols):
            off = 0
            for i, n in enumerate(kv_lens):
                kt_scr[h, :, off:off + n] = kv_refs[2 * i][:, cols].T
                v_scr[h, off:off + n, 0:LANES] = kv_refs[2 * i + 1][:, cols]
                off += n
            v_scr[h, :, LANES:2 * LANES] = jnp.ones((off, LANES), BF16)

    lp = lamp_ref[...]
    lam = (jnp.exp(jnp.sum(lp[0:1] * lp[1:2], axis=-1, keepdims=True))
           - jnp.exp(jnp.sum(lp[2:3] * lp[3:4], axis=-1, keepdims=True)) + lam_init)
    rows = min(ATT_UNIT_ROWS, q_ref.shape[0])
    units = [(h, u) for h in range(heads) for u in range(q_ref.shape[0] // rows)]

    def scores(unit):
        h, u = unit
        q = q_ref[u * rows:(u + 1) * rows, head_cols[h]]
        lane = lax.broadcasted_iota(jnp.int32, q.shape, 1)
        zero = jnp.zeros_like(q)
        qq = jnp.concatenate([jnp.where(lane < DIFF_HEAD_DIM, q, zero),
                              jnp.where(lane < DIFF_HEAD_DIM, zero, q)], axis=0)
        return jnp.dot(qq, kt_scr[h], preferred_element_type=F32)

    s_next = scores(units[0])
    for n, (h, u) in enumerate(units):
        s = s_next
        if n + 1 < len(units):
            s_next = scores(units[n + 1])
        e = jnp.exp2(s - s.max(axis=-1, keepdims=True)).astype(BF16)
        acc1 = jnp.dot(e[:rows], v_scr[h], preferred_element_type=F32)
        acc2 = jnp.dot(e[rows:], v_scr[h], preferred_element_type=F32)
        o = (acc1[:, 0:LANES] / acc1[:, LANES:2 * LANES]
             - lam * (acc2[:, 0:LANES] / acc2[:, LANES:2 * LANES]))
        o_ref[u * rows:(u + 1) * rows, head_cols[h]] = (
            _rms(o) * gsub_ref[...] * (1.0 - lam_init)).astype(BF16)


def _attention(q_arr, q_col, kvs, lamp, gsub, *, batch, q_len, tq, heads, lam_init):
    nqb = q_len // tq
    width = heads * LANES
    assert all(c % heads == 0 for c in [q_col] + [kv[1] for kv in kvs] + [kv[2] for kv in kvs])
    in_specs = [pl.BlockSpec((tq, width), lambda b, h, i: (b * nqb + i, q_col // heads + h))]
    args = [q_arr]
    for arr, k_col, v_col, kv_len in kvs:
        in_specs.append(pl.BlockSpec((kv_len, width), lambda b, h, i, c=k_col // heads: (b, c + h)))
        in_specs.append(pl.BlockSpec((kv_len, width), lambda b, h, i, c=v_col // heads: (b, c + h)))
        args += [arr, arr]
    in_specs += [pl.BlockSpec(lamp.shape, lambda b, h, i: (0, 0)),
                 pl.BlockSpec(gsub.shape, lambda b, h, i: (0, 0))]
    args += [lamp, gsub]
    kv_lens = tuple(kv[3] for kv in kvs)
    return pl.pallas_call(
        functools.partial(_attn_kernel, kv_lens=kv_lens, heads=heads, lam_init=lam_init),
        grid=(batch, DIFF_HEADS // heads, nqb),
        in_specs=in_specs,
        out_specs=pl.BlockSpec((tq, width), lambda b, h, i: (b * nqb + i, h)),
        out_shape=jax.ShapeDtypeStruct((batch * q_len, ATT_W), BF16),
        scratch_shapes=[pltpu.VMEM((heads, LANES, sum(kv_lens)), BF16),
                        pltpu.VMEM((heads, sum(kv_lens), 2 * LANES), BF16)],
        compiler_params=_params("arbitrary", "arbitrary", "arbitrary"),
        name="diff_attn",
    )(*args)


FOURIER_BLK = 128


def _dft_constants(seq_len):
    half = seq_len // 2
    k = np.arange(half, dtype=np.int64)
    ang = 2.0 * np.pi * ((k[:, None] * k[None, :]) % seq_len).astype(np.float64) / seq_len
    ch = np.cos(ang).astype(np.float32)
    sh = np.sin(ang).astype(np.float32)
    c = np.arange(FOURIER_GW, dtype=np.int64)
    angc = 2.0 * np.pi * ((c[:, None] * c[None, :]) % FOURIER_GW).astype(np.float64) / FOURIER_GW
    norm = 1.0 / math.sqrt(seq_len * FOURIER_GW)
    cc = (np.cos(angc) * norm).astype(np.float32)
    sc = (np.sin(angc) * norm).astype(np.float32)
    perm = np.zeros((FOURIER_BLK, 2 * FOURIER_BLK), np.float32)
    r = np.arange(1, FOURIER_BLK)
    perm[r, FOURIER_BLK - r] = 1.0
    perm[0, FOURIER_BLK] = 1.0
    alt = np.zeros((SUBLANES, half), np.float32)
    alt[0] = 1.0 - 2.0 * (k % 2)
    return tuple(jnp.asarray(t).astype(BF16) for t in (ch, sh, cc, sc, perm, alt))


def _fourier_kernel(u_ref, wf_ref, cc_ref, sc_ref, ch_ref, sh_ref, perm_ref, alt_ref, o_ref, r1_scr, r2_scr):
    seq_len = u_ref.shape[0]
    half = seq_len // 2
    blk = FOURIER_BLK
    nb = half // blk
    groups = [slice(g * FOURIER_GW, (g + 1) * FOURIER_GW) for g in range(FOURIER_GROUPS)]
    perm = perm_ref[...]
    rev = perm[:, 0:blk]
    mix_a, mix_b = [], []
    for g in groups:
        wf = wf_ref[g.start // FOURIER_GW].astype(BF16)
        mix_a.append(jnp.dot(cc_ref[...], wf, preferred_element_type=F32).astype(BF16))
        mix_b.append(jnp.dot(sc_ref[...], wf, preferred_element_type=F32).astype(BF16))

    for i in range(nb):
        if i == 0:
            u_rev = jnp.dot(rev, u_ref[seq_len - blk:seq_len, :], preferred_element_type=F32)
        else:
            u_rev = jnp.dot(perm, u_ref[seq_len - (i + 1) * blk:seq_len - (i - 1) * blk, :],
                            preferred_element_type=F32)
        u_blk = u_ref[i * blk:(i + 1) * blk, :].astype(F32)
        ue = (u_blk + u_rev).astype(BF16)
        uo = (u_blk - u_rev).astype(BF16)
        for g, a, b in zip(groups, mix_a, mix_b):
            r1_scr[i * blk:(i + 1) * blk, g] = jnp.dot(ue[:, g], a, preferred_element_type=F32).astype(BF16)
            r2_scr[i * blk:(i + 1) * blk, g] = jnp.dot(uo[:, g], b, preferred_element_type=F32).astype(BF16)

    u_mid = u_ref[half:half + SUBLANES, :]
    t_mid = jnp.concatenate([jnp.dot(u_mid[:, g], a, preferred_element_type=F32)
                             for g, a in zip(groups, mix_a)], axis=1)[0:1]
    k = lax.broadcasted_iota(jnp.int32, (half, 1), 0)
    sign = jnp.where(k % 2 == 0, 1.0, -1.0).astype(F32)
    p = jnp.dot(ch_ref[...], r1_scr[...], preferred_element_type=F32) + sign * t_mid
    q = jnp.dot(sh_ref[...], r2_scr[...], preferred_element_type=F32)
    o_ref[0:half, :] = (p - q).astype(BF16)
    z = (p + q).astype(BF16)
    y_mid = jnp.dot(alt_ref[...], r1_scr[...], preferred_element_type=F32)[0:1] + t_mid
    row = lax.broadcasted_iota(jnp.int32, (blk, 1), 0)
    for i in range(nb):
        if i == 0:
            w = jnp.dot(rev, z[half - blk:half], preferred_element_type=F32)
            w = jnp.where(row == 0, y_mid, w)
        else:
            w = jnp.dot(perm, z[half - (i + 1) * blk:half - (i - 1) * blk], preferred_element_type=F32)
        o_ref[half + i * blk:half + (i + 1) * blk, :] = w.astype(BF16)


def _fourier(p, w_f, layer, *, batch, seq_len):
    ch, sh, cc, sc, perm, alt = _dft_constants(seq_len)
    half = seq_len // 2
    full = lambda t: pl.BlockSpec(t.shape, lambda b: (0,) * t.ndim)
    return pl.pallas_call(
        _fourier_kernel,
        grid=(batch,),
        in_specs=[pl.BlockSpec((seq_len, FOURIER_W), lambda b: (b, 0)),
                  _layer_spec(w_f, layer, lambda b: (0, 0, 0)),
                  full(cc), full(sc), full(ch), full(sh), full(perm), full(alt)],
        out_specs=pl.BlockSpec((seq_len, FOURIER_W), lambda b: (b, 0)),
        out_shape=jax.ShapeDtypeStruct((batch * seq_len, FOURIER_W), BF16),
        scratch_shapes=[pltpu.VMEM((half, FOURIER_W), BF16), pltpu.VMEM((half, FOURIER_W), BF16)],
        compiler_params=_params("arbitrary"),
        name="fourier",
    )(p, w_f, cc, sc, ch, sh, perm, alt)


CONV_PAD = 16
CONV_ROWS = 128
CONV_LEAD = CONV_PAD - CONV_K // 2


def _conv_kernel(a_ref, gt_ref, wdw_ref, bdw_ref, gln_ref, bln_ref, wpw_ref, bpw_ref, o_ref, zp_scr):
    seq_len = a_ref.shape[0]
    zp_scr[0:CONV_PAD, :] = jnp.zeros((CONV_PAD, CONV_W), F32)
    zp_scr[CONV_PAD + seq_len:2 * CONV_PAD + seq_len, :] = jnp.zeros((CONV_PAD, CONV_W), F32)
    zp_scr[CONV_PAD:CONV_PAD + seq_len, :] = (a_ref[...].astype(F32)
                                              * jax.nn.sigmoid(gt_ref[...].astype(F32)))
    span = CONV_ROWS + 2 * CONV_PAD - SUBLANES

    def body(r, carry):
        base = pl.multiple_of(r * CONV_ROWS, CONV_ROWS)
        cols = []
        for c in range(CONV_W // LANES):
            win = zp_scr[pl.ds(base, CONV_ROWS + 2 * CONV_PAD), c * LANES:(c + 1) * LANES]
            acc = None
            for shift in range(SUBLANES):
                shifted = win if shift == 0 else pltpu.roll(win, win.shape[0] - shift, 0)
                for a in range(span // SUBLANES):
                    k = SUBLANES * a + shift - CONV_LEAD
                    if 0 <= k < CONV_K and SUBLANES * a + CONV_ROWS <= span:
                        term = (shifted[SUBLANES * a:SUBLANES * a + CONV_ROWS]
                                * wdw_ref[k:k + 1, c * LANES:(c + 1) * LANES])
                        acc = term if acc is None else acc + term
            cols.append(acc)
        z = jnp.concatenate(cols, axis=1) + bdw_ref[...]
        zc = z - jnp.mean(z, axis=-1, keepdims=True)
        y = zc * lax.rsqrt(jnp.mean(zc * zc, axis=-1, keepdims=True) + EPS) * gln_ref[...] + bln_ref[...]
        s = (y * jax.nn.sigmoid(y)).astype(BF16)
        out = jnp.dot(s, wpw_ref[...], preferred_element_type=F32) + bpw_ref[...]
        o_ref[pl.ds(base, CONV_ROWS), :] = out.astype(BF16)
        return carry

    lax.fori_loop(0, seq_len // CONV_ROWS, body, 0)


def _conv(p, w_dw, b_dw, g_ln, b_ln, w_pw, layer, b_pw, *, batch, seq_len):
    vec = pl.BlockSpec((1, CONV_W), lambda b: (0, 0))
    return pl.pallas_call(
        _conv_kernel,
        grid=(batch,),
        in_specs=[pl.BlockSpec((seq_len, CONV_W), lambda b: (b, 1)),
                  pl.BlockSpec((seq_len, CONV_W), lambda b: (b, 2)),
                  pl.BlockSpec(w_dw.shape, lambda b: (0, 0)),
                  vec, vec, vec,
                  _layer_spec(w_pw, layer, lambda b: (0, 0)),
                  vec],
        out_specs=pl.BlockSpec((seq_len, CONV_W), lambda b: (b, 0)),
        out_shape=jax.ShapeDtypeStruct((batch * seq_len, CONV_W), BF16),
        scratch_shapes=[pltpu.VMEM((seq_len + 2 * CONV_PAD, CONV_W), F32)],
        compiler_params=_params("arbitrary"),
        name="conv_module",
    )(p, p, w_dw, b_dw.reshape(1, -1), g_ln.reshape(1, -1), b_ln.reshape(1, -1), w_pw, b_pw.reshape(1, -1))


def _out_proj_kernel(yf_ref, yc_ref, ao_ref, w_ref, x_ref, gate_ref, gpost_ref, o_ref, w_scr):
    @pl.when(pl.program_id(0) == 0)
    def _():
        w_scr[...] = w_ref[...].astype(BF16)

    y = jnp.dot(yf_ref[...], w_scr[0:FOURIER_W, :], preferred_element_type=F32)
    y = y + jnp.dot(yc_ref[...], w_scr[FOURIER_W:FOURIER_W + CONV_W, :], preferred_element_type=F32)
    y = y + jnp.dot(ao_ref[...], w_scr[FOURIER_W + CONV_W:, :], preferred_element_type=F32)
    o_ref[...] = x_ref[...] + gate_ref[...] * (_rms(y) * gpost_ref[...])


def _out_proj(yf, yc, ao, w, layer, x, modv, g_post, *, group_rows, tm):
    m, d = x.shape
    bpg = group_rows // tm
    return pl.pallas_call(
        _out_proj_kernel,
        grid=(m // tm,),
        in_specs=[pl.BlockSpec((tm, FOURIER_W), lambda i: (i, 0)),
                  pl.BlockSpec((tm, CONV_W), lambda i: (i, 0)),
                  pl.BlockSpec((tm, ATT_W), lambda i: (i, 0)),
                  _layer_spec(w, layer, lambda i: (0, 0), pipeline_mode=pl.Buffered(1)),
                  pl.BlockSpec((tm, d), lambda i: (i, 0)),
                  pl.BlockSpec((None, None, 1, d), lambda i: (i // bpg, 2, 0, 0)),
                  pl.BlockSpec((1, d), lambda i: (0, 0))],
        out_specs=pl.BlockSpec((tm, d), lambda i: (i, 0)),
        out_shape=jax.ShapeDtypeStruct((m, d), F32),
        scratch_shapes=[pltpu.VMEM(w.shape[1:], BF16)],
        compiler_params=_params("arbitrary"),
        name="out_proj",
    )(yf, yc, ao, w, x, modv, g_post.reshape(1, d))


def _mlp_kernel(x_ref, g_ref, sh_ref, sc_ref, gate_ref, gpost_ref, w1_ref, w2_ref, o_ref, h_scr):
    j = pl.program_id(1)
    last = pl.num_programs(1) - 1

    def ff_chunk(hb):
        h1 = jnp.maximum(jnp.dot(hb, w1_ref[...].astype(BF16), preferred_element_type=F32), 0.0)
        return jnp.dot((h1 * h1).astype(BF16), w2_ref[...].astype(BF16), preferred_element_type=F32)

    @pl.when(j == 0)
    def _():
        h = _rms(x_ref[...]) * g_ref[...]
        hb = (h * (1.0 + sc_ref[...]) + sh_ref[...]).astype(BF16)
        h_scr[...] = hb
        o_ref[...] = ff_chunk(hb)

    @pl.when(jnp.logical_and(j > 0, j < last))
    def _():
        o_ref[...] += ff_chunk(h_scr[...])

    @pl.when(j == last)
    def _():
        y = o_ref[...] + ff_chunk(h_scr[...])
        o_ref[...] = x_ref[...] + gate_ref[...] * (_rms(y) * gpost_ref[...])


def _mlp(x, g_pre, modv, g_post, w1, w2, layer, *, group_rows, tm, tf):
    m, d = x.shape
    dff = w1.shape[-1]
    bpg = group_rows // tm
    mod_spec = lambda k: pl.BlockSpec((None, None, 1, d), lambda i, j: (i // bpg, k, 0, 0))
    return pl.pallas_call(
        _mlp_kernel,
        grid=(m // tm, dff // tf),
        in_specs=[pl.BlockSpec((tm, d), lambda i, j: (i, 0)),
                  pl.BlockSpec((1, d), lambda i, j: (0, 0)),
                  mod_spec(3), mod_spec(4), mod_spec(5),
                  pl.BlockSpec((1, d), lambda i, j: (0, 0)),
                  _layer_spec(w1, layer, lambda i, j: (0, j), block_tail=(d, tf)),
                  _layer_spec(w2, layer, lambda i, j: (j, 0), block_tail=(tf, d))],
        out_specs=pl.BlockSpec((tm, d), lambda i, j: (i, 0)),
        out_shape=jax.ShapeDtypeStruct((m, d), F32),
        scratch_shapes=[pltpu.VMEM((tm, d), BF16)],
        compiler_params=pltpu.CompilerParams(dimension_semantics=("arbitrary", "arbitrary"),
                                             vmem_limit_bytes=V7X_VMEM_LIMIT_MLP_BYTES),
        name="mlp",
    )(x, g_pre.reshape(1, d), modv, modv, modv, g_post.reshape(1, d), w1, w2)


def _rope_tables(n):
    rows = n // GRID_W
    row = jnp.repeat(jnp.arange(rows), GRID_W).astype(F32)
    col = jnp.tile(jnp.arange(GRID_W), rows).astype(F32)
    n_freq = DIFF_HEAD_DIM // 4
    inv = ROPE_BASE ** (-jnp.arange(n_freq, dtype=F32) / n_freq)
    ang = jnp.concatenate([row[:, None] * inv, col[:, None] * inv], axis=-1)
    cos, sin = jnp.cos(ang), jnp.sin(ang)
    return (jnp.concatenate([cos, cos, cos, cos], axis=-1),
            jnp.concatenate([-sin, sin, -sin, sin], axis=-1))


def kernel(x, c, ctx, c_ctx, w_ada, b_ada, g_pre_mix, g_post_mix, g_pre_mlp, g_post_mlp, w_in, w_out, w_fourier, w_dw, b_dw, g_conv_ln, b_conv_ln, w_conv_pw, b_conv_pw, lambda_q1, lambda_k1, lambda_q2, lambda_k2, g_subln, w_mlp_in, w_mlp_out):
    bsz, n, d = x.shape
    n_ctx = ctx.shape[1]
    depth = w_ada.shape[0]

    cvec = jnp.zeros((8, d), F32).at[:bsz].set(c).at[bsz].set(c_ctx)
    mod = _adaln(cvec, w_ada, b_ada)

    w_pw_b = w_conv_pw.astype(BF16)
    cos, sin = _rope_tables(n)

    xl = x.reshape(bsz * n, d)
    xc = ctx.reshape(bsz * n_ctx, d)
    q_col = Q_CHUNKS[0] * CHUNK // LANES
    k_col = K_CHUNKS[0] * CHUNK // LANES
    v_col = KV_CHUNKS[2] * CHUNK // LANES

    for l in range(depth):
        last = l == depth - 1
        lam_init = 0.8 - 0.6 * math.exp(-0.3 * l)
        mod_lat = mod[l, :bsz].reshape(bsz, 6, 1, d)
        mod_ctx = mod[l, bsz:bsz + 1].reshape(1, 6, 1, d)
        lamp = jnp.zeros((8, LANES), F32)
        lamp = lamp.at[0, :DIFF_HEAD_DIM].set(lambda_q1[l]).at[1, :DIFF_HEAD_DIM].set(lambda_k1[l])
        lamp = lamp.at[2, :DIFF_HEAD_DIM].set(lambda_q2[l]).at[3, :DIFF_HEAD_DIM].set(lambda_k2[l])
        gsub = g_subln[l].reshape(1, LANES)

        p_lat = _in_proj(xl, g_pre_mix[l], mod_lat, w_in, l, cos, sin,
                         chunks=ALL_CHUNKS, group_rows=n, tm=512)
        ctx_chunks = KV_CHUNKS if last else ALL_CHUNKS
        p_ctx = _in_proj(xc, g_pre_mix[l], mod_ctx, w_in, l, None, None,
                         chunks=ctx_chunks, group_rows=bsz * n_ctx, tm=512)
        kc_col = ctx_chunks.index(K_CHUNKS[0]) * CHUNK // LANES
        vc_col = ctx_chunks.index(KV_CHUNKS[2]) * CHUNK // LANES

        ao = _attention(p_lat, q_col, [(p_ctx, kc_col, vc_col, n_ctx), (p_lat, k_col, v_col, n)],
                        lamp, gsub, batch=bsz, q_len=n, tq=n, heads=1, lam_init=lam_init)
        yf = _fourier(p_lat, w_fourier, l, batch=bsz, seq_len=n)
        yc = _conv(p_lat, w_dw[l], b_dw[l], g_conv_ln[l], b_conv_ln[l], w_pw_b, l, b_conv_pw[l],
                   batch=bsz, seq_len=n)
        xl_mid = _out_proj(yf, yc, ao, w_out, l, xl, mod_lat, g_post_mix[l], group_rows=n, tm=512)

        if not last:
            aoc = _attention(p_ctx, q_col, [(p_ctx, kc_col, vc_col, n_ctx)], lamp, gsub,
                             batch=bsz, q_len=n_ctx, tq=n_ctx, heads=4, lam_init=lam_init)
            yfc = _fourier(p_ctx, w_fourier, l, batch=bsz, seq_len=n_ctx)
            ycc = _conv(p_ctx, w_dw[l], b_dw[l], g_conv_ln[l], b_conv_ln[l], w_pw_b, l, b_conv_pw[l],
                        batch=bsz, seq_len=n_ctx)
            xc_mid = _out_proj(yfc, ycc, aoc, w_out, l, xc, mod_ctx, g_post_mix[l],
                               group_rows=bsz * n_ctx, tm=512)
            xc = _mlp(xc_mid, g_pre_mlp[l], mod_ctx, g_post_mlp[l], w_mlp_in, w_mlp_out, l,
                      group_rows=bsz * n_ctx, tm=1024, tf=512)

        xl = _mlp(xl_mid, g_pre_mlp[l], mod_lat, g_post_mlp[l], w_mlp_in, w_mlp_out, l,
                  group_rows=n, tm=1024, tf=512)

    return xl.reshape(bsz, n, d)
```

```python
import functools
import math

import numpy as np
import jax
import jax.numpy as jnp
from jax import lax
from jax.experimental import pallas as pl
from jax.experimental.pallas import tpu as pltpu

F32 = jnp.float32
BF16 = jnp.bfloat16

D_MODEL = 2048
GRID_W = 64
FOURIER_W = 512
FOURIER_GROUPS = 4
FOURIER_GW = 128
CONV_W = 512
CONV_K = 31
ATT_W = 1024
DIFF_HEAD_DIM = 64
DIFF_HEADS = 8
D_FF = 4 * D_MODEL
ROPE_BASE = 10000.0
EPS = 1e-6
IN_COLS = 4608

CHUNK = 512
Q_CHUNKS = (3, 4)
K_CHUNKS = (5, 6)
ALL_CHUNKS = tuple(range(IN_COLS // CHUNK))
KV_CHUNKS = (5, 6, 7, 8)

Q_SCALE = DIFF_HEAD_DIM ** -0.5 * math.log2(math.e)

LANES = 128
SUBLANES = 8
V7X_VMEM_LIMIT_BYTES = 56 * 1024 * 1024
V7X_VMEM_LIMIT_MLP_BYTES = 62 * 1024 * 1024


def _params(*semantics):
    return pltpu.CompilerParams(dimension_semantics=semantics,
                                vmem_limit_bytes=V7X_VMEM_LIMIT_BYTES)


def _rms(y):
    return y * lax.rsqrt(jnp.mean(y * y, axis=-1, keepdims=True) + EPS)


def _layer_spec(w, layer, index_map_tail, block_tail=None, **kw):
    block_tail = w.shape[1:] if block_tail is None else block_tail
    return pl.BlockSpec((None,) + tuple(block_tail), lambda *g: (layer,) + tuple(index_map_tail(*g)), **kw)


def _adaln_kernel(c_ref, w_ref, b_ref, o_ref):
    s = jax.nn.silu(c_ref[...]).astype(BF16)
    o_ref[...] = jnp.dot(s, w_ref[...].astype(BF16), preferred_element_type=F32) + b_ref[...]


def _adaln(cvec, w_ada, b_ada):
    depth, d, n6 = w_ada.shape
    tn = 1024
    return pl.pallas_call(
        _adaln_kernel,
        grid=(depth, n6 // tn),
        in_specs=[pl.BlockSpec((8, d), lambda l, j: (0, 0)),
                  pl.BlockSpec((None, d, tn), lambda l, j: (l, 0, j)),
                  pl.BlockSpec((None, 1, tn), lambda l, j: (l, 0, j))],
        out_specs=pl.BlockSpec((None, 8, tn), lambda l, j: (l, 0, j)),
        out_shape=jax.ShapeDtypeStruct((depth, 8, n6), F32),
        compiler_params=_params("arbitrary", "arbitrary"),
        name="adaln",
    )(cvec, w_ada, b_ada.reshape(depth, 1, n6))


def _in_proj_kernel(*refs, chunks, rope, layer):
    if rope:
        x_ref, g_ref, sh_ref, sc_ref, w_hbm, cos_ref, sin_ref, o_ref, w_scr, stage, sem = refs
    else:
        x_ref, g_ref, sh_ref, sc_ref, w_hbm, o_ref, w_scr, stage, sem = refs

    def chunk_copy(n):
        slot = n % 2
        return pltpu.make_async_copy(w_hbm.at[layer, :, pl.ds(chunks[n] * CHUNK, CHUNK)],
                                     stage.at[slot], sem.at[slot])

    def normed():
        h = _rms(x_ref[...]) * g_ref[...]
        return (h * (1.0 + sc_ref[...]) + sh_ref[...]).astype(BF16)

    def project(hb, n):
        j = chunks[n]
        acc = jnp.dot(hb, w_scr[:, n * CHUNK:(n + 1) * CHUNK], preferred_element_type=F32)
        if j in Q_CHUNKS or j in K_CHUNKS:
            if rope:
                cos = cos_ref[...]
                sin = sin_ref[...]
                lane = lax.broadcasted_iota(jnp.int32, cos.shape, 1)
                first_half = (lane % DIFF_HEAD_DIM) < (DIFF_HEAD_DIM // 2)
            for s in range(CHUNK // LANES):
                t = acc[:, s * LANES:(s + 1) * LANES]
                if rope:
                    swapped = jnp.where(first_half,
                                        pltpu.roll(t, LANES - DIFF_HEAD_DIM // 2, 1),
                                        pltpu.roll(t, DIFF_HEAD_DIM // 2, 1))
                    t = t * cos + swapped * sin
                if j in Q_CHUNKS:
                    t = t * Q_SCALE
                o_ref[:, n * CHUNK + s * LANES:n * CHUNK + (s + 1) * LANES] = t.astype(BF16)
        else:
            o_ref[:, n * CHUNK:(n + 1) * CHUNK] = acc.astype(BF16)

    @pl.when(pl.program_id(0) == 0)
    def _():
        chunk_copy(0).start()
        hb = normed()
        for n in range(len(chunks)):
            if n + 1 < len(chunks):
                chunk_copy(n + 1).start()
            chunk_copy(n).wait()
            w_scr[:, n * CHUNK:(n + 1) * CHUNK] = stage[n % 2].astype(BF16)
            project(hb, n)

    @pl.when(pl.program_id(0) != 0)
    def _():
        hb = normed()
        for n in range(len(chunks)):
            project(hb, n)


def _in_proj(x, g_pre, modv, w, layer, cos, sin, *, chunks, group_rows, tm):
    m, d = x.shape
    bpg = group_rows // tm
    rope = cos is not None
    in_specs = [pl.BlockSpec((tm, d), lambda i: (i, 0)),
                pl.BlockSpec((1, d), lambda i: (0, 0)),
                pl.BlockSpec((None, None, 1, d), lambda i: (i // bpg, 0, 0, 0)),
                pl.BlockSpec((None, None, 1, d), lambda i: (i // bpg, 1, 0, 0)),
                pl.BlockSpec(memory_space=pl.ANY)]
    args = [x, g_pre.reshape(1, d), modv, modv, w]
    if rope:
        in_specs += [pl.BlockSpec((tm, LANES), lambda i: (i % bpg, 0)),
                     pl.BlockSpec((tm, LANES), lambda i: (i % bpg, 0))]
        args += [cos, sin]
    ncol = len(chunks) * CHUNK
    return pl.pallas_call(
        functools.partial(_in_proj_kernel, chunks=chunks, rope=rope, layer=layer),
        grid=(m // tm,),
        in_specs=in_specs,
        out_specs=pl.BlockSpec((tm, ncol), lambda i: (i, 0)),
        out_shape=jax.ShapeDtypeStruct((m, ncol), BF16),
        scratch_shapes=[pltpu.VMEM((d, ncol), BF16),
                        pltpu.VMEM((2, d, CHUNK), F32),
                        pltpu.SemaphoreType.DMA((2,))],
        compiler_params=_params("arbitrary"),
        name="in_proj",
    )(*args)


ATT_UNIT_ROWS = 512


def _attn_kernel(*refs, kv_lens, heads, lam_init, conv_rows):
    n_kv = len(kv_lens)
    q_ref = refs[0]
    kv_refs = refs[1:1 + 2 * n_kv]
    if conv_rows:
        (lamp_ref, gsub_ref, a_ref, gt_ref, wdw_ref, bdw_ref, gln_ref, bln_ref, wpw_ref, bpw_ref,
         o_ref, yc_ref, kt_scr, v_scr, zp_scr) = refs[1 + 2 * n_kv:]

        @pl.when(jnp.logical_and(pl.program_id(1) == 0, pl.program_id(2) == 0))
        def _():
            _conv_fill(a_ref, gt_ref, zp_scr)

        conv_blocks = list(range(conv_rows // CONV_ROWS))
        conv_base = (pl.program_id(1) * pl.num_programs(2) + pl.program_id(2)) * conv_rows
    else:
        lamp_ref, gsub_ref, o_ref, kt_scr, v_scr = refs[1 + 2 * n_kv:]
        conv_blocks = []
    head_cols = [slice(h * LANES, (h + 1) * LANES) for h in range(heads)]

    @pl.when(pl.program_id(2) == 0)
    def _():
        for h, cols in enumerate(head_cols):
            off = 0
            for i, n in enumerate(kv_lens):
                kt_scr[h, :, off:off + n] = kv_refs[2 * i][:, cols].T
                v_scr[h, off:off + n, 0:LANES] = kv_refs[2 * i + 1][:, cols]
                off += n
            v_scr[h, :, LANES:2 * LANES] = jnp.ones((off, LANES), BF16)

    lp = lamp_ref[...]
    lam = (jnp.exp(jnp.sum(lp[0:1] * lp[1:2], axis=-1, keepdims=True))
           - jnp.exp(jnp.sum(lp[2:3] * lp[3:4], axis=-1, keepdims=True)) + lam_init)
    rows = min(ATT_UNIT_ROWS, q_ref.shape[0])
    units = [(h, u) for h in range(heads) for u in range(q_ref.shape[0] // rows)]

    def scores(unit):
        h, u = unit
        q = q_ref[u * rows:(u + 1) * rows, head_cols[h]]
        lane = lax.broadcasted_iota(jnp.int32, q.shape, 1)
        zero = jnp.zeros_like(q)
        qq = jnp.concatenate([jnp.where(lane < DIFF_HEAD_DIM, q, zero),
                              jnp.where(lane < DIFF_HEAD_DIM, zero, q)], axis=0)
        return jnp.dot(qq, kt_scr[h], preferred_element_type=F32)

    s_next = scores(units[0])
    for n, (h, u) in enumerate(units):
        s = s_next
        if n + 1 < len(units):
            s_next = scores(units[n + 1])
        e = jnp.exp2(s - s.max(axis=-1, keepdims=True)).astype(BF16)
        acc1 = jnp.dot(e[:rows], v_scr[h], preferred_element_type=F32)
        acc2 = jnp.dot(e[rows:], v_scr[h], preferred_element_type=F32)
        o = (acc1[:, 0:LANES] / acc1[:, LANES:2 * LANES]
             - lam * (acc2[:, 0:LANES] / acc2[:, LANES:2 * LANES]))
        o_ref[u * rows:(u + 1) * rows, head_cols[h]] = (
            _rms(o) * gsub_ref[...] * (1.0 - lam_init)).astype(BF16)
        lo, hi = (n * len(conv_blocks) // len(units), (n + 1) * len(conv_blocks) // len(units))
        for c in conv_blocks[lo:hi]:
            base = pl.multiple_of(conv_base + c * CONV_ROWS, CONV_ROWS)
            yc_ref[c * CONV_ROWS:(c + 1) * CONV_ROWS, :] = _conv_block(
                zp_scr, base, wdw_ref, bdw_ref, gln_ref, bln_ref, wpw_ref, bpw_ref)


def _attention(q_arr, q_col, kvs, lamp, gsub, *, batch, q_len, tq, heads, lam_init, conv=None):
    nqb = q_len // tq
    width = heads * LANES
    n_head_steps = DIFF_HEADS // heads
    assert all(c % heads == 0 for c in [q_col] + [kv[1] for kv in kvs] + [kv[2] for kv in kvs])
    in_specs = [pl.BlockSpec((tq, width), lambda b, h, i: (b * nqb + i, q_col // heads + h))]
    args = [q_arr]
    for arr, k_col, v_col, kv_len in kvs:
        in_specs.append(pl.BlockSpec((kv_len, width), lambda b, h, i, c=k_col // heads: (b, c + h)))
        in_specs.append(pl.BlockSpec((kv_len, width), lambda b, h, i, c=v_col // heads: (b, c + h)))
        args += [arr, arr]
    in_specs += [pl.BlockSpec(lamp.shape, lambda b, h, i: (0, 0)),
                 pl.BlockSpec(gsub.shape, lambda b, h, i: (0, 0))]
    args += [lamp, gsub]
    kv_lens = tuple(kv[3] for kv in kvs)
    out_specs = pl.BlockSpec((tq, width), lambda b, h, i: (b * nqb + i, h))
    out_shape = jax.ShapeDtypeStruct((batch * q_len, ATT_W), BF16)
    scratch = [pltpu.VMEM((heads, LANES, sum(kv_lens)), BF16),
               pltpu.VMEM((heads, sum(kv_lens), 2 * LANES), BF16)]
    conv_rows = 0
    if conv is not None:
        p, w_dw, b_dw, g_ln, b_ln, w_pw, layer, b_pw = conv
        conv_rows = q_len // (n_head_steps * nqb)
        assert conv_rows % CONV_ROWS == 0
        vec = pl.BlockSpec((1, CONV_W), lambda b, h, i: (0, 0))
        in_specs += [pl.BlockSpec((q_len, CONV_W), lambda b, h, i: (b, 1)),
                     pl.BlockSpec((q_len, CONV_W), lambda b, h, i: (b, 2)),
                     pl.BlockSpec(w_dw.shape, lambda b, h, i: (0, 0)),
                     vec, vec, vec,
                     _layer_spec(w_pw, layer, lambda b, h, i: (0, 0)),
                     vec]
        args += [p, p, w_dw, b_dw.reshape(1, -1), g_ln.reshape(1, -1), b_ln.reshape(1, -1), w_pw,
                 b_pw.reshape(1, -1)]
        out_specs = [out_specs,
                     pl.BlockSpec((conv_rows, CONV_W), lambda b, h, i: ((b * n_head_steps + h) * nqb + i, 0))]
        out_shape = [out_shape, jax.ShapeDtypeStruct((batch * q_len, CONV_W), BF16)]
        scratch.append(pltpu.VMEM((q_len + 2 * CONV_PAD, CONV_W), F32))
    return pl.pallas_call(
        functools.partial(_attn_kernel, kv_lens=kv_lens, heads=heads, lam_init=lam_init, conv_rows=conv_rows),
        grid=(batch, n_head_steps, nqb),
        in_specs=in_specs,
        out_specs=out_specs,
        out_shape=out_shape,
        scratch_shapes=scratch,
        compiler_params=_params("arbitrary", "arbitrary", "arbitrary"),
        name="diff_attn",
    )(*args)


FOURIER_BLK = 128


def _dft_constants(seq_len):
    half = seq_len // 2
    k = np.arange(half, dtype=np.int64)
    ang = 2.0 * np.pi * ((k[:, None] * k[None, :]) % seq_len).astype(np.float64) / seq_len
    ch = np.cos(ang).astype(np.float32)
    sh = np.sin(ang).astype(np.float32)
    c = np.arange(FOURIER_GW, dtype=np.int64)
    angc = 2.0 * np.pi * ((c[:, None] * c[None, :]) % FOURIER_GW).astype(np.float64) / FOURIER_GW
    norm = 1.0 / math.sqrt(seq_len * FOURIER_GW)
    cc = (np.cos(angc) * norm).astype(np.float32)
    sc = (np.sin(angc) * norm).astype(np.float32)
    perm = np.zeros((FOURIER_BLK, 2 * FOURIER_BLK), np.float32)
    r = np.arange(1, FOURIER_BLK)
    perm[r, FOURIER_BLK - r] = 1.0
    perm[0, FOURIER_BLK] = 1.0
    alt = np.zeros((SUBLANES, half), np.float32)
    alt[0] = 1.0 - 2.0 * (k % 2)
    return tuple(jnp.asarray(t).astype(BF16) for t in (ch, sh, cc, sc, perm, alt))


def _fourier_kernel(u_ref, wf_ref, cc_ref, sc_ref, ch_ref, sh_ref, perm_ref, alt_ref, o_ref, r1_scr, r2_scr):
    seq_len = u_ref.shape[0]
    half = seq_len // 2
    blk = FOURIER_BLK
    nb = half // blk
    groups = [slice(g * FOURIER_GW, (g + 1) * FOURIER_GW) for g in range(FOURIER_GROUPS)]
    perm = perm_ref[...]
    rev = perm[:, 0:blk]
    mix_a, mix_b = [], []
    for g in groups:
        wf = wf_ref[g.start // FOURIER_GW].astype(BF16)
        mix_a.append(jnp.dot(cc_ref[...], wf, preferred_element_type=F32).astype(BF16))
        mix_b.append(jnp.dot(sc_ref[...], wf, preferred_element_type=F32).astype(BF16))

    for i in range(nb):
        if i == 0:
            u_rev = jnp.dot(rev, u_ref[seq_len - blk:seq_len, :], preferred_element_type=F32)
        else:
            u_rev = jnp.dot(perm, u_ref[seq_len - (i + 1) * blk:seq_len - (i - 1) * blk, :],
                            preferred_element_type=F32)
        u_blk = u_ref[i * blk:(i + 1) * blk, :].astype(F32)
        ue = (u_blk + u_rev).astype(BF16)
        uo = (u_blk - u_rev).astype(BF16)
        for g, a, b in zip(groups, mix_a, mix_b):
            r1_scr[i * blk:(i + 1) * blk, g] = jnp.dot(ue[:, g], a, preferred_element_type=F32).astype(BF16)
            r2_scr[i * blk:(i + 1) * blk, g] = jnp.dot(uo[:, g], b, preferred_element_type=F32).astype(BF16)

    u_mid = u_ref[half:half + SUBLANES, :]
    t_mid = jnp.concatenate([jnp.dot(u_mid[:, g], a, preferred_element_type=F32)
                             for g, a in zip(groups, mix_a)], axis=1)[0:1]
    k = lax.broadcasted_iota(jnp.int32, (half, 1), 0)
    sign = jnp.where(k % 2 == 0, 1.0, -1.0).astype(F32)
    p = jnp.dot(ch_ref[...], r1_scr[...], preferred_element_type=F32) + sign * t_mid
    q = jnp.dot(sh_ref[...], r2_scr[...], preferred_element_type=F32)
    o_ref[0:half, :] = (p - q).astype(BF16)
    z = (p + q).astype(BF16)
    y_mid = jnp.dot(alt_ref[...], r1_scr[...], preferred_element_type=F32)[0:1] + t_mid
    row = lax.broadcasted_iota(jnp.int32, (blk, 1), 0)
    for i in range(nb):
        if i == 0:
            w = jnp.dot(rev, z[half - blk:half], preferred_element_type=F32)
            w = jnp.where(row == 0, y_mid, w)
        else:
            w = jnp.dot(perm, z[half - (i + 1) * blk:half - (i - 1) * blk], preferred_element_type=F32)
        o_ref[half + i * blk:half + (i + 1) * blk, :] = w.astype(BF16)


def _fourier(p, w_f, layer, *, batch, seq_len):
    ch, sh, cc, sc, perm, alt = _dft_constants(seq_len)
    half = seq_len // 2
    full = lambda t: pl.BlockSpec(t.shape, lambda b: (0,) * t.ndim)
    return pl.pallas_call(
        _fourier_kernel,
        grid=(batch,),
        in_specs=[pl.BlockSpec((seq_len, FOURIER_W), lambda b: (b, 0)),
                  _layer_spec(w_f, layer, lambda b: (0, 0, 0)),
                  full(cc), full(sc), full(ch), full(sh), full(perm), full(alt)],
        out_specs=pl.BlockSpec((seq_len, FOURIER_W), lambda b: (b, 0)),
        out_shape=jax.ShapeDtypeStruct((batch * seq_len, FOURIER_W), BF16),
        scratch_shapes=[pltpu.VMEM((half, FOURIER_W), BF16), pltpu.VMEM((half, FOURIER_W), BF16)],
        compiler_params=_params("arbitrary"),
        name="fourier",
    )(p, w_f, cc, sc, ch, sh, perm, alt)


CONV_PAD = 16
CONV_ROWS = 128
CONV_LEAD = CONV_PAD - CONV_K // 2


def _conv_fill(a_ref, gt_ref, zp_scr):
    seq_len = a_ref.shape[0]
    zp_scr[0:CONV_PAD, :] = jnp.zeros((CONV_PAD, CONV_W), F32)
    zp_scr[CONV_PAD + seq_len:2 * CONV_PAD + seq_len, :] = jnp.zeros((CONV_PAD, CONV_W), F32)
    zp_scr[CONV_PAD:CONV_PAD + seq_len, :] = (a_ref[...].astype(F32)
                                              * jax.nn.sigmoid(gt_ref[...].astype(F32)))


def _conv_block(zp_scr, base, wdw_ref, bdw_ref, gln_ref, bln_ref, wpw_ref, bpw_ref):
    span = CONV_ROWS + 2 * CONV_PAD - SUBLANES
    cols = []
    for c in range(CONV_W // LANES):
        win = zp_scr[pl.ds(base, CONV_ROWS + 2 * CONV_PAD), c * LANES:(c + 1) * LANES]
        acc = None
        for shift in range(SUBLANES):
            shifted = win if shift == 0 else pltpu.roll(win, win.shape[0] - shift, 0)
            for a in range(span // SUBLANES):
                k = SUBLANES * a + shift - CONV_LEAD
                if 0 <= k < CONV_K and SUBLANES * a + CONV_ROWS <= span:
                    term = (shifted[SUBLANES * a:SUBLANES * a + CONV_ROWS]
                            * wdw_ref[k:k + 1, c * LANES:(c + 1) * LANES])
                    acc = term if acc is None else acc + term
        cols.append(acc)
    z = jnp.concatenate(cols, axis=1) + bdw_ref[...]
    zc = z - jnp.mean(z, axis=-1, keepdims=True)
    y = zc * lax.rsqrt(jnp.mean(zc * zc, axis=-1, keepdims=True) + EPS) * gln_ref[...] + bln_ref[...]
    s = (y * jax.nn.sigmoid(y)).astype(BF16)
    return (jnp.dot(s, wpw_ref[...], preferred_element_type=F32) + bpw_ref[...]).astype(BF16)


def _conv_kernel(a_ref, gt_ref, wdw_ref, bdw_ref, gln_ref, bln_ref, wpw_ref, bpw_ref, o_ref, zp_scr):
    _conv_fill(a_ref, gt_ref, zp_scr)

    def body(r, carry):
        base = pl.multiple_of(r * CONV_ROWS, CONV_ROWS)
        o_ref[pl.ds(base, CONV_ROWS), :] = _conv_block(zp_scr, base, wdw_ref, bdw_ref, gln_ref, bln_ref,
                                                       wpw_ref, bpw_ref)
        return carry

    lax.fori_loop(0, a_ref.shape[0] // CONV_ROWS, body, 0)


def _conv(p, w_dw, b_dw, g_ln, b_ln, w_pw, layer, b_pw, *, batch, seq_len):
    vec = pl.BlockSpec((1, CONV_W), lambda b: (0, 0))
    return pl.pallas_call(
        _conv_kernel,
        grid=(batch,),
        in_specs=[pl.BlockSpec((seq_len, CONV_W), lambda b: (b, 1)),
                  pl.BlockSpec((seq_len, CONV_W), lambda b: (b, 2)),
                  pl.BlockSpec(w_dw.shape, lambda b: (0, 0)),
                  vec, vec, vec,
                  _layer_spec(w_pw, layer, lambda b: (0, 0)),
                  vec],
        out_specs=pl.BlockSpec((seq_len, CONV_W), lambda b: (b, 0)),
        out_shape=jax.ShapeDtypeStruct((batch * seq_len, CONV_W), BF16),
        scratch_shapes=[pltpu.VMEM((seq_len + 2 * CONV_PAD, CONV_W), F32)],
        compiler_params=_params("arbitrary"),
        name="conv_module",
    )(p, p, w_dw, b_dw.reshape(1, -1), g_ln.reshape(1, -1), b_ln.reshape(1, -1), w_pw, b_pw.reshape(1, -1))


def _out_proj_kernel(yf_ref, yc_ref, ao_ref, w_ref, x_ref, gate_ref, gpost_ref, o_ref, w_scr):
    @pl.when(pl.program_id(0) == 0)
    def _():
        w_scr[...] = w_ref[...].astype(BF16)

    y = jnp.dot(yf_ref[...], w_scr[0:FOURIER_W, :], preferred_element_type=F32)
    y = y + jnp.dot(yc_ref[...], w_scr[FOURIER_W:FOURIER_W + CONV_W, :], preferred_element_type=F32)
    y = y + jnp.dot(ao_ref[...], w_scr[FOURIER_W + CONV_W:, :], preferred_element_type=F32)
    o_ref[...] = x_ref[...] + gate_ref[...] * (_rms(y) * gpost_ref[...])


def _out_proj(yf, yc, ao, w, layer, x, modv, g_post, *, group_rows, tm):
    m, d = x.shape
    bpg = group_rows // tm
    return pl.pallas_call(
        _out_proj_kernel,
        grid=(m // tm,),
        in_specs=[pl.BlockSpec((tm, FOURIER_W), lambda i: (i, 0)),
                  pl.BlockSpec((tm, CONV_W), lambda i: (i, 0)),
                  pl.BlockSpec((tm, ATT_W), lambda i: (i, 0)),
                  _layer_spec(w, layer, lambda i: (0, 0), pipeline_mode=pl.Buffered(1)),
                  pl.BlockSpec((tm, d), lambda i: (i, 0)),
                  pl.BlockSpec((None, None, 1, d), lambda i: (i // bpg, 2, 0, 0)),
                  pl.BlockSpec((1, d), lambda i: (0, 0))],
        out_specs=pl.BlockSpec((tm, d), lambda i: (i, 0)),
        out_shape=jax.ShapeDtypeStruct((m, d), F32),
        scratch_shapes=[pltpu.VMEM(w.shape[1:], BF16)],
        compiler_params=_params("arbitrary"),
        name="out_proj",
    )(yf, yc, ao, w, x, modv, g_post.reshape(1, d))


def _mlp_kernel(x_ref, g_ref, sh_ref, sc_ref, gate_ref, gpost_ref, w1_ref, w2_ref, o_ref, h_scr):
    j = pl.program_id(1)
    last = pl.num_programs(1) - 1

    def ff_chunk(hb):
        h1 = jnp.maximum(jnp.dot(hb, w1_ref[...].astype(BF16), preferred_element_type=F32), 0.0)
        return jnp.dot((h1 * h1).astype(BF16), w2_ref[...].astype(BF16), preferred_element_type=F32)

    @pl.when(j == 0)
    def _():
        h = _rms(x_ref[...]) * g_ref[...]
        hb = (h * (1.0 + sc_ref[...]) + sh_ref[...]).astype(BF16)
        h_scr[...] = hb
        o_ref[...] = ff_chunk(hb)

    @pl.when(jnp.logical_and(j > 0, j < last))
    def _():
        o_ref[...] += ff_chunk(h_scr[...])

    @pl.when(j == last)
    def _():
        y = o_ref[...] + ff_chunk(h_scr[...])
        o_ref[...] = x_ref[...] + gate_ref[...] * (_rms(y) * gpost_ref[...])


def _mlp(x, g_pre, modv, g_post, w1, w2, layer, *, group_rows, tm, tf):
    m, d = x.shape
    dff = w1.shape[-1]
    bpg = group_rows // tm
    mod_spec = lambda k: pl.BlockSpec((None, None, 1, d), lambda i, j: (i // bpg, k, 0, 0))
    return pl.pallas_call(
        _mlp_kernel,
        grid=(m // tm, dff // tf),
        in_specs=[pl.BlockSpec((tm, d), lambda i, j: (i, 0)),
                  pl.BlockSpec((1, d), lambda i, j: (0, 0)),
                  mod_spec(3), mod_spec(4), mod_spec(5),
                  pl.BlockSpec((1, d), lambda i, j: (0, 0)),
                  _layer_spec(w1, layer, lambda i, j: (0, j), block_tail=(d, tf)),
                  _layer_spec(w2, layer, lambda i, j: (j, 0), block_tail=(tf, d))],
        out_specs=pl.BlockSpec((tm, d), lambda i, j: (i, 0)),
        out_shape=jax.ShapeDtypeStruct((m, d), F32),
        scratch_shapes=[pltpu.VMEM((tm, d), BF16)],
        compiler_params=pltpu.CompilerParams(dimension_semantics=("arbitrary", "arbitrary"),
                                             vmem_limit_bytes=V7X_VMEM_LIMIT_MLP_BYTES),
        name="mlp",
    )(x, g_pre.reshape(1, d), modv, modv, modv, g_post.reshape(1, d), w1, w2)


def _rope_tables(n):
    rows = n // GRID_W
    row = jnp.repeat(jnp.arange(rows), GRID_W).astype(F32)
    col = jnp.tile(jnp.arange(GRID_W), rows).astype(F32)
    n_freq = DIFF_HEAD_DIM // 4
    inv = ROPE_BASE ** (-jnp.arange(n_freq, dtype=F32) / n_freq)
    ang = jnp.concatenate([row[:, None] * inv, col[:, None] * inv], axis=-1)
    cos, sin = jnp.cos(ang), jnp.sin(ang)
    return (jnp.concatenate([cos, cos, cos, cos], axis=-1),
            jnp.concatenate([-sin, sin, -sin, sin], axis=-1))


def kernel(x, c, ctx, c_ctx, w_ada, b_ada, g_pre_mix, g_post_mix, g_pre_mlp, g_post_mlp, w_in, w_out, w_fourier, w_dw, b_dw, g_conv_ln, b_conv_ln, w_conv_pw, b_conv_pw, lambda_q1, lambda_k1, lambda_q2, lambda_k2, g_subln, w_mlp_in, w_mlp_out):
    bsz, n, d = x.shape
    n_ctx = ctx.shape[1]
    depth = w_ada.shape[0]

    cvec = jnp.zeros((8, d), F32).at[:bsz].set(c).at[bsz].set(c_ctx)
    mod = _adaln(cvec, w_ada, b_ada)

    w_pw_b = w_conv_pw.astype(BF16)
    cos, sin = _rope_tables(n)

    xl = x.reshape(bsz * n, d)
    xc = ctx.reshape(bsz * n_ctx, d)
    q_col = Q_CHUNKS[0] * CHUNK // LANES
    k_col = K_CHUNKS[0] * CHUNK // LANES
    v_col = KV_CHUNKS[2] * CHUNK // LANES

    for l in range(depth):
        last = l == depth - 1
        lam_init = 0.8 - 0.6 * math.exp(-0.3 * l)
        mod_lat = mod[l, :bsz].reshape(bsz, 6, 1, d)
        mod_ctx = mod[l, bsz:bsz + 1].reshape(1, 6, 1, d)
        lamp = jnp.zeros((8, LANES), F32)
        lamp = lamp.at[0, :DIFF_HEAD_DIM].set(lambda_q1[l]).at[1, :DIFF_HEAD_DIM].set(lambda_k1[l])
        lamp = lamp.at[2, :DIFF_HEAD_DIM].set(lambda_q2[l]).at[3, :DIFF_HEAD_DIM].set(lambda_k2[l])
        gsub = g_subln[l].reshape(1, LANES)

        p_lat = _in_proj(xl, g_pre_mix[l], mod_lat, w_in, l, cos, sin,
                         chunks=ALL_CHUNKS, group_rows=n, tm=512)
        ctx_chunks = KV_CHUNKS if last else ALL_CHUNKS
        p_ctx = _in_proj(xc, g_pre_mix[l], mod_ctx, w_in, l, None, None,
                         chunks=ctx_chunks, group_rows=bsz * n_ctx, tm=512)
        kc_col = ctx_chunks.index(K_CHUNKS[0]) * CHUNK // LANES
        vc_col = ctx_chunks.index(KV_CHUNKS[2]) * CHUNK // LANES

        conv_args = (w_dw[l], b_dw[l], g_conv_ln[l], b_conv_ln[l], w_pw_b, l, b_conv_pw[l])
        ao, yc = _attention(p_lat, q_col, [(p_ctx, kc_col, vc_col, n_ctx), (p_lat, k_col, v_col, n)],
                            lamp, gsub, batch=bsz, q_len=n, tq=n, heads=1, lam_init=lam_init,
                            conv=(p_lat,) + conv_args)
        yf = _fourier(p_lat, w_fourier, l, batch=bsz, seq_len=n)
        xl_mid = _out_proj(yf, yc, ao, w_out, l, xl, mod_lat, g_post_mix[l], group_rows=n, tm=512)

        if not last:
            aoc = _attention(p_ctx, q_col, [(p_ctx, kc_col, vc_col, n_ctx)], lamp, gsub,
                             batch=bsz, q_len=n_ctx, tq=n_ctx, heads=4, lam_init=lam_init)
            yfc = _fourier(p_ctx, w_fourier, l, batch=bsz, seq_len=n_ctx)
            ycc = _conv(p_ctx, *conv_args, batch=bsz, seq_len=n_ctx)
            xc_mid = _out_proj(yfc, ycc, aoc, w_out, l, xc, mod_ctx, g_post_mix[l],
                               group_rows=bsz * n_ctx, tm=512)
            xc = _mlp(xc_mid, g_pre_mlp[l], mod_ctx, g_post_mlp[l], w_mlp_in, w_mlp_out, l,
                      group_rows=bsz * n_ctx, tm=1024, tf=512)

        xl = _mlp(xl_mid, g_pre_mlp[l], mod_lat, g_post_mlp[l], w_mlp_in, w_mlp_out, l,
                  group_rows=n, tm=1024, tf=512)

    return xl.reshape(bsz, n, d)
```

```python
import functools
import math

import numpy as np
import jax
import jax.numpy as jnp
from jax import lax
from jax.experimental import pallas as pl
from jax.experimental.pallas import tpu as pltpu

F32 = jnp.float32
BF16 = jnp.bfloat16

D_MODEL = 2048
GRID_W = 64
FOURIER_W = 512
FOURIER_GROUPS = 4
FOURIER_GW = 128
CONV_W = 512
CONV_K = 31
ATT_W = 1024
DIFF_HEAD_DIM = 64
DIFF_HEADS = 8
D_FF = 4 * D_MODEL
ROPE_BASE = 10000.0
EPS = 1e-6
IN_COLS = 4608

CHUNK = 512
Q_CHUNKS = (3, 4)
K_CHUNKS = (5, 6)
ALL_CHUNKS = tuple(range(IN_COLS // CHUNK))
KV_CHUNKS = (5, 6, 7, 8)

Q_SCALE = DIFF_HEAD_DIM ** -0.5 * math.log2(math.e)

LANES = 128
SUBLANES = 8
V7X_VMEM_LIMIT_BYTES = 56 * 1024 * 1024
V7X_VMEM_LIMIT_MLP_BYTES = 62 * 1024 * 1024


def _params(*semantics):
    return pltpu.CompilerParams(dimension_semantics=semantics,
                                vmem_limit_bytes=V7X_VMEM_LIMIT_BYTES)


def _rms(y):
    return y * lax.rsqrt(jnp.mean(y * y, axis=-1, keepdims=True) + EPS)


def _mod_spec(d, layer, row0, blocks_per_row, k):
    return pl.BlockSpec((None, None, None, 1, d), lambda i, *_: (layer, row0 + i // blocks_per_row, k, 0, 0))


def _layer_spec(w, layer, index_map_tail, block_tail=None, **kw):
    block_tail = w.shape[1:] if block_tail is None else block_tail
    return pl.BlockSpec((None,) + tuple(block_tail), lambda *g: (layer,) + tuple(index_map_tail(*g)), **kw)


def _adaln_kernel(c_ref, w_ref, b_ref, o_ref):
    s = jax.nn.silu(c_ref[...]).astype(BF16)
    o_ref[...] = jnp.dot(s, w_ref[...].astype(BF16), preferred_element_type=F32) + b_ref[...]


def _adaln(cvec, w_ada, b_ada):
    depth, d, n6 = w_ada.shape
    tn = 1024
    return pl.pallas_call(
        _adaln_kernel,
        grid=(depth, n6 // tn),
        in_specs=[pl.BlockSpec((8, d), lambda l, j: (0, 0)),
                  pl.BlockSpec((None, d, tn), lambda l, j: (l, 0, j)),
                  pl.BlockSpec((None, 1, tn), lambda l, j: (l, 0, j))],
        out_specs=pl.BlockSpec((None, 8, tn), lambda l, j: (l, 0, j)),
        out_shape=jax.ShapeDtypeStruct((depth, 8, n6), F32),
        compiler_params=_params("arbitrary", "arbitrary"),
        name="adaln",
    )(cvec, w_ada, b_ada.reshape(depth, 1, n6))


def _in_proj_kernel(*refs, chunks, rope, layer):
    if rope:
        x_ref, g_ref, sh_ref, sc_ref, w_hbm, cos_ref, sin_ref, o_ref, w_scr, stage, sem = refs
    else:
        x_ref, g_ref, sh_ref, sc_ref, w_hbm, o_ref, w_scr, stage, sem = refs

    def chunk_copy(n):
        slot = n % 2
        return pltpu.make_async_copy(w_hbm.at[layer, :, pl.ds(chunks[n] * CHUNK, CHUNK)],
                                     stage.at[slot], sem.at[slot])

    def normed():
        h = _rms(x_ref[...]) * g_ref[...]
        return (h * (1.0 + sc_ref[...]) + sh_ref[...]).astype(BF16)

    def project(hb, n):
        j = chunks[n]
        acc = jnp.dot(hb, w_scr[:, n * CHUNK:(n + 1) * CHUNK], preferred_element_type=F32)
        if j in Q_CHUNKS or j in K_CHUNKS:
            if rope:
                cos = cos_ref[...]
                sin = sin_ref[...]
                lane = lax.broadcasted_iota(jnp.int32, cos.shape, 1)
                first_half = (lane % DIFF_HEAD_DIM) < (DIFF_HEAD_DIM // 2)
            for s in range(CHUNK // LANES):
                t = acc[:, s * LANES:(s + 1) * LANES]
                if rope:
                    swapped = jnp.where(first_half,
                                        pltpu.roll(t, LANES - DIFF_HEAD_DIM // 2, 1),
                                        pltpu.roll(t, DIFF_HEAD_DIM // 2, 1))
                    t = t * cos + swapped * sin
                if j in Q_CHUNKS:
                    t = t * Q_SCALE
                o_ref[:, n * CHUNK + s * LANES:n * CHUNK + (s + 1) * LANES] = t.astype(BF16)
        else:
            o_ref[:, n * CHUNK:(n + 1) * CHUNK] = acc.astype(BF16)

    @pl.when(pl.program_id(0) == 0)
    def _():
        chunk_copy(0).start()
        hb = normed()
        for n in range(len(chunks)):
            if n + 1 < len(chunks):
                chunk_copy(n + 1).start()
            chunk_copy(n).wait()
            w_scr[:, n * CHUNK:(n + 1) * CHUNK] = stage[n % 2].astype(BF16)
            project(hb, n)

    @pl.when(pl.program_id(0) != 0)
    def _():
        hb = normed()
        for n in range(len(chunks)):
            project(hb, n)


def _in_proj(x, g_pre, modv, row0, w, layer, cos, sin, *, chunks, group_rows, tm):
    m, d = x.shape
    bpg = group_rows // tm
    rope = cos is not None
    in_specs = [pl.BlockSpec((tm, d), lambda i: (i, 0)),
                pl.BlockSpec((1, d), lambda i: (0, 0)),
                _mod_spec(d, layer, row0, bpg, 0),
                _mod_spec(d, layer, row0, bpg, 1),
                pl.BlockSpec(memory_space=pl.ANY)]
    args = [x, g_pre.reshape(1, d), modv, modv, w]
    if rope:
        in_specs += [pl.BlockSpec((tm, LANES), lambda i: (i % bpg, 0)),
                     pl.BlockSpec((tm, LANES), lambda i: (i % bpg, 0))]
        args += [cos, sin]
    ncol = len(chunks) * CHUNK
    return pl.pallas_call(
        functools.partial(_in_proj_kernel, chunks=chunks, rope=rope, layer=layer),
        grid=(m // tm,),
        in_specs=in_specs,
        out_specs=pl.BlockSpec((tm, ncol), lambda i: (i, 0)),
        out_shape=jax.ShapeDtypeStruct((m, ncol), BF16),
        scratch_shapes=[pltpu.VMEM((d, ncol), BF16),
                        pltpu.VMEM((2, d, CHUNK), F32),
                        pltpu.SemaphoreType.DMA((2,))],
        compiler_params=_params("arbitrary"),
        name="in_proj",
    )(*args)


ATT_UNIT_ROWS = 512


def _attn_kernel(*refs, kv_lens, heads, lam_init, conv_rows):
    n_kv = len(kv_lens)
    q_ref = refs[0]
    kv_refs = refs[1:1 + 2 * n_kv]
    if conv_rows:
        (lamp_ref, gsub_ref, a_ref, gt_ref, wdw_ref, bdw_ref, gln_ref, bln_ref, wpw_ref, bpw_ref,
         o_ref, yc_ref, kt_scr, v_scr, zp_scr) = refs[1 + 2 * n_kv:]

        @pl.when(jnp.logical_and(pl.program_id(1) == 0, pl.program_id(2) == 0))
        def _():
            _conv_fill(a_ref, gt_ref, zp_scr)

        conv_blocks = list(range(conv_rows // CONV_ROWS))
        conv_base = (pl.program_id(1) * pl.num_programs(2) + pl.program_id(2)) * conv_rows
    else:
        lamp_ref, gsub_ref, o_ref, kt_scr, v_scr = refs[1 + 2 * n_kv:]
        conv_blocks = []
    head_cols = [slice(h * LANES, (h + 1) * LANES) for h in range(heads)]

    @pl.when(pl.program_id(2) == 0)
    def _():
        for h, cols in enumerate(head_cols):
            off = 0
            for i, n in enumerate(kv_lens):
                kt_scr[h, :, off:off + n] = kv_refs[2 * i][:, cols].T
                v_scr[h, off:off + n, 0:LANES] = kv_refs[2 * i + 1][:, cols]
                off += n
            v_scr[h, :, LANES:2 * LANES] = jnp.ones((off, LANES), BF16)

    lp = lamp_ref[...]
    lam = (jnp.exp(jnp.sum(lp[0:1] * lp[1:2], axis=-1, keepdims=True))
           - jnp.exp(jnp.sum(lp[2:3] * lp[3:4], axis=-1, keepdims=True)) + lam_init)
    rows = min(ATT_UNIT_ROWS, q_ref.shape[0])
    units = [(h, u) for h in range(heads) for u in range(q_ref.shape[0] // rows)]

    def scores(unit):
        h, u = unit
        q = q_ref[u * rows:(u + 1) * rows, head_cols[h]]
        lane = lax.broadcasted_iota(jnp.int32, q.shape, 1)
        zero = jnp.zeros_like(q)
        qq = jnp.concatenate([jnp.where(lane < DIFF_HEAD_DIM, q, zero),
                              jnp.where(lane < DIFF_HEAD_DIM, zero, q)], axis=0)
        return jnp.dot(qq, kt_scr[h], preferred_element_type=F32)

    s_next = scores(units[0])
    for n, (h, u) in enumerate(units):
        s = s_next
        if n + 1 < len(units):
            s_next = scores(units[n + 1])
        e = jnp.exp2(s - s.max(axis=-1, keepdims=True)).astype(BF16)
        acc1 = jnp.dot(e[:rows], v_scr[h], preferred_element_type=F32)
        acc2 = jnp.dot(e[rows:], v_scr[h], preferred_element_type=F32)
        o = (acc1[:, 0:LANES] / acc1[:, LANES:2 * LANES]
             - lam * (acc2[:, 0:LANES] / acc2[:, LANES:2 * LANES]))
        o_ref[u * rows:(u + 1) * rows, head_cols[h]] = (
            _rms(o) * gsub_ref[...] * (1.0 - lam_init)).astype(BF16)
        for c in [c for c in conv_blocks if c * len(units) // len(conv_blocks) == n]:
            base = pl.multiple_of(conv_base + c * CONV_ROWS, CONV_ROWS)
            yc_ref[c * CONV_ROWS:(c + 1) * CONV_ROWS, :] = _conv_block(
                zp_scr, base, wdw_ref, bdw_ref, gln_ref, bln_ref, wpw_ref, bpw_ref)


def _attention(q_arr, q_col, kvs, lamp, gsub, *, batch, q_len, tq, heads, lam_init, conv=None):
    nqb = q_len // tq
    width = heads * LANES
    n_head_steps = DIFF_HEADS // heads
    assert all(c % heads == 0 for c in [q_col] + [kv[1] for kv in kvs] + [kv[2] for kv in kvs])
    in_specs = [pl.BlockSpec((tq, width), lambda b, h, i: (b * nqb + i, q_col // heads + h))]
    args = [q_arr]
    for arr, k_col, v_col, kv_len in kvs:
        in_specs.append(pl.BlockSpec((kv_len, width), lambda b, h, i, c=k_col // heads: (b, c + h)))
        in_specs.append(pl.BlockSpec((kv_len, width), lambda b, h, i, c=v_col // heads: (b, c + h)))
        args += [arr, arr]
    in_specs += [pl.BlockSpec(lamp.shape, lambda b, h, i: (0, 0)),
                 pl.BlockSpec(gsub.shape, lambda b, h, i: (0, 0))]
    args += [lamp, gsub]
    kv_lens = tuple(kv[3] for kv in kvs)
    out_specs = pl.BlockSpec((tq, width), lambda b, h, i: (b * nqb + i, h))
    out_shape = jax.ShapeDtypeStruct((batch * q_len, ATT_W), BF16)
    scratch = [pltpu.VMEM((heads, LANES, sum(kv_lens)), BF16),
               pltpu.VMEM((heads, sum(kv_lens), 2 * LANES), BF16)]
    conv_rows = 0
    if conv is not None:
        p, w_dw, b_dw, g_ln, b_ln, w_pw, layer, b_pw = conv
        conv_rows = q_len // (n_head_steps * nqb)
        assert conv_rows % CONV_ROWS == 0
        vec = pl.BlockSpec((1, CONV_W), lambda b, h, i: (0, 0))
        in_specs += [pl.BlockSpec((q_len, CONV_W), lambda b, h, i: (b, 1)),
                     pl.BlockSpec((q_len, CONV_W), lambda b, h, i: (b, 2)),
                     pl.BlockSpec(w_dw.shape, lambda b, h, i: (0, 0)),
                     vec, vec, vec,
                     _layer_spec(w_pw, layer, lambda b, h, i: (0, 0)),
                     vec]
        args += [p, p, w_dw, b_dw.reshape(1, -1), g_ln.reshape(1, -1), b_ln.reshape(1, -1), w_pw,
                 b_pw.reshape(1, -1)]
        out_specs = [out_specs,
                     pl.BlockSpec((conv_rows, CONV_W), lambda b, h, i: ((b * n_head_steps + h) * nqb + i, 0))]
        out_shape = [out_shape, jax.ShapeDtypeStruct((batch * q_len, CONV_W), BF16)]
        scratch.append(pltpu.VMEM((q_len + 2 * CONV_PAD, CONV_W), F32))
    return pl.pallas_call(
        functools.partial(_attn_kernel, kv_lens=kv_lens, heads=heads, lam_init=lam_init, conv_rows=conv_rows),
        grid=(batch, n_head_steps, nqb),
        in_specs=in_specs,
        out_specs=out_specs,
        out_shape=out_shape,
        scratch_shapes=scratch,
        compiler_params=_params("arbitrary", "arbitrary", "arbitrary"),
        name="diff_attn",
    )(*args)


FOURIER_BLK = 128


def _dft_constants(seq_len):
    half = seq_len // 2
    k = np.arange(half, dtype=np.int64)
    ang = 2.0 * np.pi * ((k[:, None] * k[None, :]) % seq_len).astype(np.float64) / seq_len
    ch = np.cos(ang).astype(np.float32)
    sh = np.sin(ang).astype(np.float32)
    c = np.arange(FOURIER_GW, dtype=np.int64)
    angc = 2.0 * np.pi * ((c[:, None] * c[None, :]) % FOURIER_GW).astype(np.float64) / FOURIER_GW
    norm = 1.0 / math.sqrt(seq_len * FOURIER_GW)
    cc = (np.cos(angc) * norm).astype(np.float32)
    sc = (np.sin(angc) * norm).astype(np.float32)
    perm = np.zeros((FOURIER_BLK, 2 * FOURIER_BLK), np.float32)
    r = np.arange(1, FOURIER_BLK)
    perm[r, FOURIER_BLK - r] = 1.0
    perm[0, FOURIER_BLK] = 1.0
    alt = np.zeros((SUBLANES, half), np.float32)
    alt[0] = 1.0 - 2.0 * (k % 2)
    return tuple(jnp.asarray(t).astype(BF16) for t in (ch, sh, cc, sc, perm, alt))


def _fourier_kernel(u_ref, wf_ref, cc_ref, sc_ref, ch_ref, sh_ref, perm_ref, alt_ref, o_ref, r1_scr, r2_scr):
    seq_len = u_ref.shape[0]
    half = seq_len // 2
    blk = FOURIER_BLK
    nb = half // blk
    groups = [slice(g * FOURIER_GW, (g + 1) * FOURIER_GW) for g in range(FOURIER_GROUPS)]
    perm = perm_ref[...]
    rev = perm[:, 0:blk]
    mix_a, mix_b = [], []
    for g in groups:
        wf = wf_ref[g.start // FOURIER_GW].astype(BF16)
        mix_a.append(jnp.dot(cc_ref[...], wf, preferred_element_type=F32).astype(BF16))
        mix_b.append(jnp.dot(sc_ref[...], wf, preferred_element_type=F32).astype(BF16))

    for i in range(nb):
        if i == 0:
            u_rev = jnp.dot(rev, u_ref[seq_len - blk:seq_len, :], preferred_element_type=F32)
        else:
            u_rev = jnp.dot(perm, u_ref[seq_len - (i + 1) * blk:seq_len - (i - 1) * blk, :],
                            preferred_element_type=F32)
        u_blk = u_ref[i * blk:(i + 1) * blk, :].astype(F32)
        ue = (u_blk + u_rev).astype(BF16)
        uo = (u_blk - u_rev).astype(BF16)
        for g, a, b in zip(groups, mix_a, mix_b):
            r1_scr[i * blk:(i + 1) * blk, g] = jnp.dot(ue[:, g], a, preferred_element_type=F32).astype(BF16)
            r2_scr[i * blk:(i + 1) * blk, g] = jnp.dot(uo[:, g], b, preferred_element_type=F32).astype(BF16)

    u_mid = u_ref[half:half + SUBLANES, :]
    t_mid = jnp.concatenate([jnp.dot(u_mid[:, g], a, preferred_element_type=F32)
                             for g, a in zip(groups, mix_a)], axis=1)[0:1]
    k = lax.broadcasted_iota(jnp.int32, (half, 1), 0)
    sign = jnp.where(k % 2 == 0, 1.0, -1.0).astype(F32)
    p = jnp.dot(ch_ref[...], r1_scr[...], preferred_element_type=F32) + sign * t_mid
    q = jnp.dot(sh_ref[...], r2_scr[...], preferred_element_type=F32)
    o_ref[0:half, :] = (p - q).astype(BF16)
    z = (p + q).astype(BF16)
    y_mid = jnp.dot(alt_ref[...], r1_scr[...], preferred_element_type=F32)[0:1] + t_mid
    row = lax.broadcasted_iota(jnp.int32, (blk, 1), 0)
    for i in range(nb):
        if i == 0:
            w = jnp.dot(rev, z[half - blk:half], preferred_element_type=F32)
            w = jnp.where(row == 0, y_mid, w)
        else:
            w = jnp.dot(perm, z[half - (i + 1) * blk:half - (i - 1) * blk], preferred_element_type=F32)
        o_ref[half + i * blk:half + (i + 1) * blk, :] = w.astype(BF16)


def _fourier(p, w_f, layer, *, batch, seq_len):
    ch, sh, cc, sc, perm, alt = _dft_constants(seq_len)
    half = seq_len // 2
    full = lambda t: pl.BlockSpec(t.shape, lambda b: (0,) * t.ndim)
    return pl.pallas_call(
        _fourier_kernel,
        grid=(batch,),
        in_specs=[pl.BlockSpec((seq_len, FOURIER_W), lambda b: (b, 0)),
                  _layer_spec(w_f, layer, lambda b: (0, 0, 0)),
                  full(cc), full(sc), full(ch), full(sh), full(perm), full(alt)],
        out_specs=pl.BlockSpec((seq_len, FOURIER_W), lambda b: (b, 0)),
        out_shape=jax.ShapeDtypeStruct((batch * seq_len, FOURIER_W), BF16),
        scratch_shapes=[pltpu.VMEM((half, FOURIER_W), BF16), pltpu.VMEM((half, FOURIER_W), BF16)],
        compiler_params=_params("arbitrary"),
        name="fourier",
    )(p, w_f, cc, sc, ch, sh, perm, alt)


CONV_PAD = 16
CONV_ROWS = 128
CONV_LEAD = CONV_PAD - CONV_K // 2


def _conv_fill(a_ref, gt_ref, zp_scr):
    seq_len = a_ref.shape[0]
    zp_scr[0:CONV_PAD, :] = jnp.zeros((CONV_PAD, CONV_W), F32)
    zp_scr[CONV_PAD + seq_len:2 * CONV_PAD + seq_len, :] = jnp.zeros((CONV_PAD, CONV_W), F32)
    zp_scr[CONV_PAD:CONV_PAD + seq_len, :] = (a_ref[...].astype(F32)
                                              * jax.nn.sigmoid(gt_ref[...].astype(F32)))


def _conv_block(zp_scr, base, wdw_ref, bdw_ref, gln_ref, bln_ref, wpw_ref, bpw_ref):
    span = CONV_ROWS + 2 * CONV_PAD - SUBLANES
    cols = []
    for c in range(CONV_W // LANES):
        win = zp_scr[pl.ds(base, CONV_ROWS + 2 * CONV_PAD), c * LANES:(c + 1) * LANES]
        acc = None
        for shift in range(SUBLANES):
            shifted = win if shift == 0 else pltpu.roll(win, win.shape[0] - shift, 0)
            for a in range(span // SUBLANES):
                k = SUBLANES * a + shift - CONV_LEAD
                if 0 <= k < CONV_K and SUBLANES * a + CONV_ROWS <= span:
                    term = (shifted[SUBLANES * a:SUBLANES * a + CONV_ROWS]
                            * wdw_ref[k:k + 1, c * LANES:(c + 1) * LANES])
                    acc = term if acc is None else acc + term
        cols.append(acc)
    z = jnp.concatenate(cols, axis=1) + bdw_ref[...]
    zc = z - jnp.mean(z, axis=-1, keepdims=True)
    y = zc * lax.rsqrt(jnp.mean(zc * zc, axis=-1, keepdims=True) + EPS) * gln_ref[...] + bln_ref[...]
    s = (y * jax.nn.sigmoid(y)).astype(BF16)
    return (jnp.dot(s, wpw_ref[...], preferred_element_type=F32) + bpw_ref[...]).astype(BF16)


def _conv_kernel(a_ref, gt_ref, wdw_ref, bdw_ref, gln_ref, bln_ref, wpw_ref, bpw_ref, o_ref, zp_scr):
    _conv_fill(a_ref, gt_ref, zp_scr)

    def body(r, carry):
        base = pl.multiple_of(r * CONV_ROWS, CONV_ROWS)
        o_ref[pl.ds(base, CONV_ROWS), :] = _conv_block(zp_scr, base, wdw_ref, bdw_ref, gln_ref, bln_ref,
                                                       wpw_ref, bpw_ref)
        return carry

    lax.fori_loop(0, a_ref.shape[0] // CONV_ROWS, body, 0)


def _conv(p, w_dw, b_dw, g_ln, b_ln, w_pw, layer, b_pw, *, batch, seq_len):
    vec = pl.BlockSpec((1, CONV_W), lambda b: (0, 0))
    return pl.pallas_call(
        _conv_kernel,
        grid=(batch,),
        in_specs=[pl.BlockSpec((seq_len, CONV_W), lambda b: (b, 1)),
                  pl.BlockSpec((seq_len, CONV_W), lambda b: (b, 2)),
                  pl.BlockSpec(w_dw.shape, lambda b: (0, 0)),
                  vec, vec, vec,
                  _layer_spec(w_pw, layer, lambda b: (0, 0)),
                  vec],
        out_specs=pl.BlockSpec((seq_len, CONV_W), lambda b: (b, 0)),
        out_shape=jax.ShapeDtypeStruct((batch * seq_len, CONV_W), BF16),
        scratch_shapes=[pltpu.VMEM((seq_len + 2 * CONV_PAD, CONV_W), F32)],
        compiler_params=_params("arbitrary"),
        name="conv_module",
    )(p, p, w_dw, b_dw.reshape(1, -1), g_ln.reshape(1, -1), b_ln.reshape(1, -1), w_pw, b_pw.reshape(1, -1))


def _out_proj_kernel(yf_ref, yc_ref, ao_ref, w_ref, x_ref, gate_ref, gpost_ref, o_ref, w_scr):
    @pl.when(pl.program_id(0) == 0)
    def _():
        w_scr[...] = w_ref[...].astype(BF16)

    y = jnp.dot(yf_ref[...], w_scr[0:FOURIER_W, :], preferred_element_type=F32)
    y = y + jnp.dot(yc_ref[...], w_scr[FOURIER_W:FOURIER_W + CONV_W, :], preferred_element_type=F32)
    y = y + jnp.dot(ao_ref[...], w_scr[FOURIER_W + CONV_W:, :], preferred_element_type=F32)
    o_ref[...] = x_ref[...] + gate_ref[...] * (_rms(y) * gpost_ref[...])


def _out_proj(yf, yc, ao, w, layer, x, modv, row0, g_post, *, group_rows, tm):
    m, d = x.shape
    bpg = group_rows // tm
    return pl.pallas_call(
        _out_proj_kernel,
        grid=(m // tm,),
        in_specs=[pl.BlockSpec((tm, FOURIER_W), lambda i: (i, 0)),
                  pl.BlockSpec((tm, CONV_W), lambda i: (i, 0)),
                  pl.BlockSpec((tm, ATT_W), lambda i: (i, 0)),
                  _layer_spec(w, layer, lambda i: (0, 0), pipeline_mode=pl.Buffered(1)),
                  pl.BlockSpec((tm, d), lambda i: (i, 0)),
                  _mod_spec(d, layer, row0, bpg, 2),
                  pl.BlockSpec((1, d), lambda i: (0, 0))],
        out_specs=pl.BlockSpec((tm, d), lambda i: (i, 0)),
        out_shape=jax.ShapeDtypeStruct((m, d), F32),
        scratch_shapes=[pltpu.VMEM(w.shape[1:], BF16)],
        compiler_params=_params("arbitrary"),
        name="out_proj",
    )(yf, yc, ao, w, x, modv, g_post.reshape(1, d))


def _mlp_kernel(x_ref, g_ref, sh_ref, sc_ref, gate_ref, gpost_ref, w1_ref, w2_ref, o_ref, h_scr):
    j = pl.program_id(1)
    last = pl.num_programs(1) - 1

    def ff_chunk(hb):
        h1 = jnp.maximum(jnp.dot(hb, w1_ref[...].astype(BF16), preferred_element_type=F32), 0.0)
        return jnp.dot((h1 * h1).astype(BF16), w2_ref[...].astype(BF16), preferred_element_type=F32)

    @pl.when(j == 0)
    def _():
        h = _rms(x_ref[...]) * g_ref[...]
        hb = (h * (1.0 + sc_ref[...]) + sh_ref[...]).astype(BF16)
        h_scr[...] = hb
        o_ref[...] = ff_chunk(hb)

    @pl.when(jnp.logical_and(j > 0, j < last))
    def _():
        o_ref[...] += ff_chunk(h_scr[...])

    @pl.when(j == last)
    def _():
        y = o_ref[...] + ff_chunk(h_scr[...])
        o_ref[...] = x_ref[...] + gate_ref[...] * (_rms(y) * gpost_ref[...])


def _mlp(x, g_pre, modv, row0, g_post, w1, w2, layer, *, group_rows, tm, tf):
    m, d = x.shape
    dff = w1.shape[-1]
    bpg = group_rows // tm
    mod_spec = lambda k: _mod_spec(d, layer, row0, bpg, k)
    return pl.pallas_call(
        _mlp_kernel,
        grid=(m // tm, dff // tf),
        in_specs=[pl.BlockSpec((tm, d), lambda i, j: (i, 0)),
                  pl.BlockSpec((1, d), lambda i, j: (0, 0)),
                  mod_spec(3), mod_spec(4), mod_spec(5),
                  pl.BlockSpec((1, d), lambda i, j: (0, 0)),
                  _layer_spec(w1, layer, lambda i, j: (0, j), block_tail=(d, tf)),
                  _layer_spec(w2, layer, lambda i, j: (j, 0), block_tail=(tf, d))],
        out_specs=pl.BlockSpec((tm, d), lambda i, j: (i, 0)),
        out_shape=jax.ShapeDtypeStruct((m, d), F32),
        scratch_shapes=[pltpu.VMEM((tm, d), BF16)],
        compiler_params=pltpu.CompilerParams(dimension_semantics=("arbitrary", "arbitrary"),
                                             vmem_limit_bytes=V7X_VMEM_LIMIT_MLP_BYTES),
        name="mlp",
    )(x, g_pre.reshape(1, d), modv, modv, modv, g_post.reshape(1, d), w1, w2)


def _rope_tables(n):
    rows = n // GRID_W
    row = np.repeat(np.arange(rows), GRID_W).astype(np.float64)
    col = np.tile(np.arange(GRID_W), rows).astype(np.float64)
    n_freq = DIFF_HEAD_DIM // 4
    inv = ROPE_BASE ** (-np.arange(n_freq, dtype=np.float64) / n_freq)
    ang = np.concatenate([row[:, None] * inv, col[:, None] * inv], axis=-1)
    cos, sin = np.cos(ang), np.sin(ang)
    return (jnp.asarray(np.concatenate([cos, cos, cos, cos], axis=-1), F32),
            jnp.asarray(np.concatenate([-sin, sin, -sin, sin], axis=-1), F32))


IN_PROJ_ROWS = 512
OUT_PROJ_ROWS = 512
MLP_ROWS = 1024
MLP_FF_CHUNK = 512
CTX_ATT_HEADS = 4


def kernel(x, c, ctx, c_ctx, w_ada, b_ada, g_pre_mix, g_post_mix, g_pre_mlp, g_post_mlp, w_in, w_out, w_fourier, w_dw, b_dw, g_conv_ln, b_conv_ln, w_conv_pw, b_conv_pw, lambda_q1, lambda_k1, lambda_q2, lambda_k2, g_subln, w_mlp_in, w_mlp_out):
    bsz, n, d = x.shape
    n_ctx = ctx.shape[1]
    depth = w_ada.shape[0]
    n_lat_rows, n_ctx_rows = bsz * n, bsz * n_ctx

    cvec = jnp.concatenate([c, c_ctx[None, :], jnp.zeros((SUBLANES - bsz - 1, d), F32)], axis=0)
    modv = _adaln(cvec, w_ada, b_ada).reshape(depth, SUBLANES, 6, 1, d)
    lamp_all = jnp.pad(jnp.stack([lambda_q1, lambda_k1, lambda_q2, lambda_k2], axis=1),
                       ((0, 0), (0, SUBLANES - 4), (0, LANES - DIFF_HEAD_DIM)))

    w_pw_b = w_conv_pw.astype(BF16)
    cos, sin = _rope_tables(n)

    xl = x.reshape(n_lat_rows, d)
    xc = ctx.reshape(n_ctx_rows, d)
    q_col = Q_CHUNKS[0] * CHUNK // LANES
    k_col = K_CHUNKS[0] * CHUNK // LANES
    v_col = KV_CHUNKS[2] * CHUNK // LANES

    for l in range(depth):
        last = l == depth - 1
        lam_init = 0.8 - 0.6 * math.exp(-0.3 * l)
        lamp = lamp_all[l]
        gsub = g_subln[l].reshape(1, LANES)

        p_lat = _in_proj(xl, g_pre_mix[l], modv, 0, w_in, l, cos, sin,
                         chunks=ALL_CHUNKS, group_rows=n, tm=IN_PROJ_ROWS)
        ctx_chunks = KV_CHUNKS if last else ALL_CHUNKS
        p_ctx = _in_proj(xc, g_pre_mix[l], modv, bsz, w_in, l, None, None,
                         chunks=ctx_chunks, group_rows=n_ctx_rows, tm=IN_PROJ_ROWS)
        kc_col = ctx_chunks.index(K_CHUNKS[0]) * CHUNK // LANES
        vc_col = ctx_chunks.index(KV_CHUNKS[2]) * CHUNK // LANES

        conv_args = (w_dw[l], b_dw[l], g_conv_ln[l], b_conv_ln[l], w_pw_b, l, b_conv_pw[l])
        ao, yc = _attention(p_lat, q_col, [(p_ctx, kc_col, vc_col, n_ctx), (p_lat, k_col, v_col, n)],
                            lamp, gsub, batch=bsz, q_len=n, tq=n, heads=1, lam_init=lam_init,
                            conv=(p_lat,) + conv_args)
        yf = _fourier(p_lat, w_fourier, l, batch=bsz, seq_len=n)
        xl_mid = _out_proj(yf, yc, ao, w_out, l, xl, modv, 0, g_post_mix[l], group_rows=n, tm=OUT_PROJ_ROWS)

        if not last:
            aoc = _attention(p_ctx, q_col, [(p_ctx, kc_col, vc_col, n_ctx)], lamp, gsub,
                             batch=bsz, q_len=n_ctx, tq=n_ctx, heads=CTX_ATT_HEADS, lam_init=lam_init)
            yfc = _fourier(p_ctx, w_fourier, l, batch=bsz, seq_len=n_ctx)
            ycc = _conv(p_ctx, *conv_args, batch=bsz, seq_len=n_ctx)
            xc_mid = _out_proj(yfc, ycc, aoc, w_out, l, xc, modv, bsz, g_post_mix[l],
                               group_rows=n_ctx_rows, tm=OUT_PROJ_ROWS)
            xc = _mlp(xc_mid, g_pre_mlp[l], modv, bsz, g_post_mlp[l], w_mlp_in, w_mlp_out, l,
                      group_rows=n_ctx_rows, tm=MLP_ROWS, tf=MLP_FF_CHUNK)

        xl = _mlp(xl_mid, g_pre_mlp[l], modv, 0, g_post_mlp[l], w_mlp_in, w_mlp_out, l,
                  group_rows=n, tm=MLP_ROWS, tf=MLP_FF_CHUNK)

    return xl.reshape(bsz, n, d)
```

```python
import functools
import math

import numpy as np
import jax
import jax.numpy as jnp
from jax import lax
from jax.experimental import pallas as pl
from jax.experimental.pallas import tpu as pltpu

F32 = jnp.float32
BF16 = jnp.bfloat16

D_MODEL = 2048
GRID_W = 64
FOURIER_W = 512
FOURIER_GROUPS = 4
FOURIER_GW = 128
CONV_W = 512
CONV_K = 31
ATT_W = 1024
DIFF_HEAD_DIM = 64
DIFF_HEADS = 8
D_FF = 4 * D_MODEL
ROPE_BASE = 10000.0
EPS = 1e-6
IN_COLS = 4608

CHUNK = 512
Q_CHUNKS = (3, 4)
K_CHUNKS = (5, 6)
ALL_CHUNKS = tuple(range(IN_COLS // CHUNK))
KV_CHUNKS = (5, 6, 7, 8)

Q_SCALE = DIFF_HEAD_DIM ** -0.5 * math.log2(math.e)

LANES = 128
SUBLANES = 8
V7X_VMEM_LIMIT_BYTES = 56 * 1024 * 1024
V7X_VMEM_LIMIT_MLP_BYTES = 62 * 1024 * 1024


def _params(*semantics):
    return pltpu.CompilerParams(dimension_semantics=semantics,
                                vmem_limit_bytes=V7X_VMEM_LIMIT_BYTES)


def _rms(y):
    return y * lax.rsqrt(jnp.mean(y * y, axis=-1, keepdims=True) + EPS)


def _mod_spec(d, layer, row0, blocks_per_row, k):
    return pl.BlockSpec((None, None, None, 1, d), lambda i, *_: (layer, row0 + i // blocks_per_row, k, 0, 0))


def _layer_spec(w, layer, index_map_tail, block_tail=None, **kw):
    block_tail = w.shape[1:] if block_tail is None else block_tail
    return pl.BlockSpec((None,) + tuple(block_tail), lambda *g: (layer,) + tuple(index_map_tail(*g)), **kw)


def _adaln_kernel(c_ref, w_ref, b_ref, o_ref):
    s = jax.nn.silu(c_ref[...]).astype(BF16)
    o_ref[...] = jnp.dot(s, w_ref[...].astype(BF16), preferred_element_type=F32) + b_ref[...]


def _adaln(cvec, w_ada, b_ada):
    depth, d, n6 = w_ada.shape
    tn = 1024
    return pl.pallas_call(
        _adaln_kernel,
        grid=(depth, n6 // tn),
        in_specs=[pl.BlockSpec((8, d), lambda l, j: (0, 0)),
                  pl.BlockSpec((None, d, tn), lambda l, j: (l, 0, j)),
                  pl.BlockSpec((None, 1, tn), lambda l, j: (l, 0, j))],
        out_specs=pl.BlockSpec((None, 8, tn), lambda l, j: (l, 0, j)),
        out_shape=jax.ShapeDtypeStruct((depth, 8, n6), F32),
        compiler_params=_params("arbitrary", "arbitrary"),
        name="adaln",
    )(cvec, w_ada, b_ada.reshape(depth, 1, n6))


CONV_A_CHUNK = 1
CONV_GATE_CHUNK = 2


def _in_proj_out_slots(chunks):
    slots, nxt = {}, 0
    for j in chunks:
        if j != CONV_A_CHUNK:
            slots[j] = nxt
            nxt += 1
    return slots


GLU_SLOT = _in_proj_out_slots(ALL_CHUNKS)[CONV_GATE_CHUNK]


def _in_proj_kernel(*refs, chunks, rope, layer):
    if rope:
        x_ref, g_ref, sh_ref, sc_ref, w_hbm, cos_ref, sin_ref, o_ref, w_scr, stage, sem = refs
    else:
        x_ref, g_ref, sh_ref, sc_ref, w_hbm, o_ref, w_scr, stage, sem = refs

    def chunk_copy(n):
        slot = n % 2
        return pltpu.make_async_copy(w_hbm.at[layer, :, pl.ds(chunks[n] * CHUNK, CHUNK)],
                                     stage.at[slot], sem.at[slot])

    def normed():
        h = _rms(x_ref[...]) * g_ref[...]
        return (h * (1.0 + sc_ref[...]) + sh_ref[...]).astype(BF16)

    def project_all(hb, fetch_chunk):
        slots = _in_proj_out_slots(chunks)
        conv_a = None
        for n, j in enumerate(chunks):
            if fetch_chunk is not None:
                fetch_chunk(n)
            acc = jnp.dot(hb, w_scr[:, n * CHUNK:(n + 1) * CHUNK], preferred_element_type=F32)
            if j == CONV_A_CHUNK:
                conv_a = acc
                continue
            col0 = slots[j] * CHUNK
            if j == CONV_GATE_CHUNK:
                o_ref[:, col0:col0 + CHUNK] = (conv_a * jax.nn.sigmoid(acc)).astype(BF16)
            elif j in Q_CHUNKS or j in K_CHUNKS:
                if rope:
                    cos = cos_ref[...]
                    sin = sin_ref[...]
                    lane = lax.broadcasted_iota(jnp.int32, cos.shape, 1)
                    first_half = (lane % DIFF_HEAD_DIM) < (DIFF_HEAD_DIM // 2)
                for s in range(CHUNK // LANES):
                    t = acc[:, s * LANES:(s + 1) * LANES]
                    if rope:
                        swapped = jnp.where(first_half,
                                            pltpu.roll(t, LANES - DIFF_HEAD_DIM // 2, 1),
                                            pltpu.roll(t, DIFF_HEAD_DIM // 2, 1))
                        t = t * cos + swapped * sin
                    if j in Q_CHUNKS:
                        t = t * Q_SCALE
                    o_ref[:, col0 + s * LANES:col0 + (s + 1) * LANES] = t.astype(BF16)
            else:
                o_ref[:, col0:col0 + CHUNK] = acc.astype(BF16)

    @pl.when(pl.program_id(0) == 0)
    def _():
        def fetch_chunk(n):
            if n + 1 < len(chunks):
                chunk_copy(n + 1).start()
            chunk_copy(n).wait()
            w_scr[:, n * CHUNK:(n + 1) * CHUNK] = stage[n % 2].astype(BF16)

        chunk_copy(0).start()
        project_all(normed(), fetch_chunk)

    @pl.when(pl.program_id(0) != 0)
    def _():
        project_all(normed(), None)


def _in_proj(x, g_pre, modv, row0, w, layer, cos, sin, *, chunks, group_rows, tm):
    m, d = x.shape
    bpg = group_rows // tm
    rope = cos is not None
    in_specs = [pl.BlockSpec((tm, d), lambda i: (i, 0)),
                pl.BlockSpec((1, d), lambda i: (0, 0)),
                _mod_spec(d, layer, row0, bpg, 0),
                _mod_spec(d, layer, row0, bpg, 1),
                pl.BlockSpec(memory_space=pl.ANY)]
    args = [x, g_pre.reshape(1, d), modv, modv, w]
    if rope:
        in_specs += [pl.BlockSpec((tm, LANES), lambda i: (i % bpg, 0)),
                     pl.BlockSpec((tm, LANES), lambda i: (i % bpg, 0))]
        args += [cos, sin]
    ncol = len(_in_proj_out_slots(chunks)) * CHUNK
    return pl.pallas_call(
        functools.partial(_in_proj_kernel, chunks=chunks, rope=rope, layer=layer),
        grid=(m // tm,),
        in_specs=in_specs,
        out_specs=pl.BlockSpec((tm, ncol), lambda i: (i, 0)),
        out_shape=jax.ShapeDtypeStruct((m, ncol), BF16),
        scratch_shapes=[pltpu.VMEM((d, len(chunks) * CHUNK), BF16),
                        pltpu.VMEM((2, d, CHUNK), F32),
                        pltpu.SemaphoreType.DMA((2,))],
        compiler_params=_params("arbitrary"),
        name="in_proj",
    )(*args)


ATT_UNIT_ROWS = 512


def _attn_kernel(*refs, kv_lens, heads, lam_init, conv_rows):
    n_kv = len(kv_lens)
    q_ref = refs[0]
    kv_refs = refs[1:1 + 2 * n_kv]
    if conv_rows:
        (lamp_ref, gsub_ref, z_ref, wdw_ref, bdw_ref, gln_ref, bln_ref, wpw_ref, bpw_ref,
         o_ref, yc_ref, kt_scr, v_scr, zp_scr) = refs[1 + 2 * n_kv:]

        @pl.when(jnp.logical_and(pl.program_id(1) == 0, pl.program_id(2) == 0))
        def _():
            _conv_fill(z_ref, zp_scr)

        conv_blocks = list(range(conv_rows // CONV_ROWS))
        conv_base = (pl.program_id(1) * pl.num_programs(2) + pl.program_id(2)) * conv_rows
    else:
        lamp_ref, gsub_ref, o_ref, kt_scr, v_scr = refs[1 + 2 * n_kv:]
        conv_blocks = []
    head_cols = [slice(h * LANES, (h + 1) * LANES) for h in range(heads)]

    @pl.when(pl.program_id(2) == 0)
    def _():
        for h, cols in enumerate(head_cols):
            off = 0
            for i, n in enumerate(kv_lens):
                kt_scr[h, :, off:off + n] = kv_refs[2 * i][:, cols].T
                v_scr[h, off:off + n, 0:LANES] = kv_refs[2 * i + 1][:, cols]
                off += n
            v_scr[h, :, LANES:2 * LANES] = jnp.ones((off, LANES), BF16)

    lp = lamp_ref[...]
    lam = (jnp.exp(jnp.sum(lp[0:1] * lp[1:2], axis=-1, keepdims=True))
           - jnp.exp(jnp.sum(lp[2:3] * lp[3:4], axis=-1, keepdims=True)) + lam_init)
    rows = min(ATT_UNIT_ROWS, q_ref.shape[0])
    units = [(h, u) for h in range(heads) for u in range(q_ref.shape[0] // rows)]

    def scores(unit):
        h, u = unit
        q = q_ref[u * rows:(u + 1) * rows, head_cols[h]]
        lane = lax.broadcasted_iota(jnp.int32, q.shape, 1)
        zero = jnp.zeros_like(q)
        qq = jnp.concatenate([jnp.where(lane < DIFF_HEAD_DIM, q, zero),
                              jnp.where(lane < DIFF_HEAD_DIM, zero, q)], axis=0)
        return jnp.dot(qq, kt_scr[h], preferred_element_type=F32)

    s_next = scores(units[0])
    for n, (h, u) in enumerate(units):
        s = s_next
        if n + 1 < len(units):
            s_next = scores(units[n + 1])
        e = jnp.exp2(s - s.max(axis=-1, keepdims=True)).astype(BF16)
        acc1 = jnp.dot(e[:rows], v_scr[h], preferred_element_type=F32)
        acc2 = jnp.dot(e[rows:], v_scr[h], preferred_element_type=F32)
        o = (acc1[:, 0:LANES] / acc1[:, LANES:2 * LANES]
             - lam * (acc2[:, 0:LANES] / acc2[:, LANES:2 * LANES]))
        o_ref[u * rows:(u + 1) * rows, head_cols[h]] = (
            _rms(o) * gsub_ref[...] * (1.0 - lam_init)).astype(BF16)
        for c in [c for c in conv_blocks if c * len(units) // len(conv_blocks) == n]:
            base = pl.multiple_of(conv_base + c * CONV_ROWS, CONV_ROWS)
            yc_ref[c * CONV_ROWS:(c + 1) * CONV_ROWS, :] = _conv_block(
                zp_scr, base, wdw_ref, bdw_ref, gln_ref, bln_ref, wpw_ref, bpw_ref)


def _attention(q_arr, q_col, kvs, lamp, gsub, *, batch, q_len, tq, heads, lam_init, conv=None):
    nqb = q_len // tq
    width = heads * LANES
    n_head_steps = DIFF_HEADS // heads
    assert all(c % heads == 0 for c in [q_col] + [kv[1] for kv in kvs] + [kv[2] for kv in kvs])
    in_specs = [pl.BlockSpec((tq, width), lambda b, h, i: (b * nqb + i, q_col // heads + h))]
    args = [q_arr]
    for arr, k_col, v_col, kv_len in kvs:
        in_specs.append(pl.BlockSpec((kv_len, width), lambda b, h, i, c=k_col // heads: (b, c + h)))
        in_specs.append(pl.BlockSpec((kv_len, width), lambda b, h, i, c=v_col // heads: (b, c + h)))
        args += [arr, arr]
    in_specs += [pl.BlockSpec(lamp.shape, lambda b, h, i: (0, 0)),
                 pl.BlockSpec(gsub.shape, lambda b, h, i: (0, 0))]
    args += [lamp, gsub]
    kv_lens = tuple(kv[3] for kv in kvs)
    out_specs = pl.BlockSpec((tq, width), lambda b, h, i: (b * nqb + i, h))
    out_shape = jax.ShapeDtypeStruct((batch * q_len, ATT_W), BF16)
    scratch = [pltpu.VMEM((heads, LANES, sum(kv_lens)), BF16),
               pltpu.VMEM((heads, sum(kv_lens), 2 * LANES), BF16)]
    conv_rows = 0
    if conv is not None:
        p, w_dw, b_dw, g_ln, b_ln, w_pw, layer, b_pw = conv
        conv_rows = q_len // (n_head_steps * nqb)
        assert conv_rows % CONV_ROWS == 0
        vec = pl.BlockSpec((1, CONV_W), lambda b, h, i: (0, 0))
        in_specs += [pl.BlockSpec((q_len, CONV_W), lambda b, h, i: (b, GLU_SLOT)),
                     pl.BlockSpec(w_dw.shape, lambda b, h, i: (0, 0)),
                     vec, vec, vec,
                     _layer_spec(w_pw, layer, lambda b, h, i: (0, 0)),
                     vec]
        args += [p, w_dw, b_dw.reshape(1, -1), g_ln.reshape(1, -1), b_ln.reshape(1, -1), w_pw,
                 b_pw.reshape(1, -1)]
        out_specs = [out_specs,
                     pl.BlockSpec((conv_rows, CONV_W), lambda b, h, i: ((b * n_head_steps + h) * nqb + i, 0))]
        out_shape = [out_shape, jax.ShapeDtypeStruct((batch * q_len, CONV_W), BF16)]
        scratch.append(pltpu.VMEM((q_len + 2 * CONV_PAD, CONV_W), F32))
    return pl.pallas_call(
        functools.partial(_attn_kernel, kv_lens=kv_lens, heads=heads, lam_init=lam_init, conv_rows=conv_rows),
        grid=(batch, n_head_steps, nqb),
        in_specs=in_specs,
        out_specs=out_specs,
        out_shape=out_shape,
        scratch_shapes=scratch,
        compiler_params=_params("arbitrary", "arbitrary", "arbitrary"),
        name="diff_attn",
    )(*args)


FOURIER_BLK = 128


def _dft_constants(seq_len):
    half = seq_len // 2
    k = np.arange(half, dtype=np.int64)
    ang = 2.0 * np.pi * ((k[:, None] * k[None, :]) % seq_len).astype(np.float64) / seq_len
    ch = np.cos(ang).astype(np.float32)
    sh = np.sin(ang).astype(np.float32)
    c = np.arange(FOURIER_GW, dtype=np.int64)
    angc = 2.0 * np.pi * ((c[:, None] * c[None, :]) % FOURIER_GW).astype(np.float64) / FOURIER_GW
    norm = 1.0 / math.sqrt(seq_len * FOURIER_GW)
    cc = (np.cos(angc) * norm).astype(np.float32)
    sc = (np.sin(angc) * norm).astype(np.float32)
    perm = np.zeros((FOURIER_BLK, 2 * FOURIER_BLK), np.float32)
    r = np.arange(1, FOURIER_BLK)
    perm[r, FOURIER_BLK - r] = 1.0
    perm[0, FOURIER_BLK] = 1.0
    alt = np.zeros((SUBLANES, half), np.float32)
    alt[0] = 1.0 - 2.0 * (k % 2)
    return tuple(jnp.asarray(t).astype(BF16) for t in (ch, sh, cc, sc, perm, alt))


def _fourier_kernel(u_ref, wf_ref, cc_ref, sc_ref, ch_ref, sh_ref, perm_ref, alt_ref, o_ref, r1_scr, r2_scr):
    seq_len = u_ref.shape[0]
    half = seq_len // 2
    blk = FOURIER_BLK
    nb = half // blk
    groups = [slice(g * FOURIER_GW, (g + 1) * FOURIER_GW) for g in range(FOURIER_GROUPS)]
    perm = perm_ref[...]
    rev = perm[:, 0:blk]
    mix_a, mix_b = [], []
    for g in groups:
        wf = wf_ref[g.start // FOURIER_GW].astype(BF16)
        mix_a.append(jnp.dot(cc_ref[...], wf, preferred_element_type=F32).astype(BF16))
        mix_b.append(jnp.dot(sc_ref[...], wf, preferred_element_type=F32).astype(BF16))

    for i in range(nb):
        if i == 0:
            u_rev = jnp.dot(rev, u_ref[seq_len - blk:seq_len, :], preferred_element_type=F32)
        else:
            u_rev = jnp.dot(perm, u_ref[seq_len - (i + 1) * blk:seq_len - (i - 1) * blk, :],
                            preferred_element_type=F32)
        u_blk = u_ref[i * blk:(i + 1) * blk, :].astype(F32)
        ue = (u_blk + u_rev).astype(BF16)
        uo = (u_blk - u_rev).astype(BF16)
        for g, a, b in zip(groups, mix_a, mix_b):
            r1_scr[i * blk:(i + 1) * blk, g] = jnp.dot(ue[:, g], a, preferred_element_type=F32).astype(BF16)
            r2_scr[i * blk:(i + 1) * blk, g] = jnp.dot(uo[:, g], b, preferred_element_type=F32).astype(BF16)

    u_mid = u_ref[half:half + SUBLANES, :]
    t_mid = jnp.concatenate([jnp.dot(u_mid[:, g], a, preferred_element_type=F32)
                             for g, a in zip(groups, mix_a)], axis=1)[0:1]
    k = lax.broadcasted_iota(jnp.int32, (half, 1), 0)
    sign = jnp.where(k % 2 == 0, 1.0, -1.0).astype(F32)
    p = jnp.dot(ch_ref[...], r1_scr[...], preferred_element_type=F32) + sign * t_mid
    q = jnp.dot(sh_ref[...], r2_scr[...], preferred_element_type=F32)
    o_ref[0:half, :] = (p - q).astype(BF16)
    z = (p + q).astype(BF16)
    y_mid = jnp.dot(alt_ref[...], r1_scr[...], preferred_element_type=F32)[0:1] + t_mid
    row = lax.broadcasted_iota(jnp.int32, (blk, 1), 0)
    for i in range(nb):
        if i == 0:
            w = jnp.dot(rev, z[half - blk:half], preferred_element_type=F32)
            w = jnp.where(row == 0, y_mid, w)
        else:
            w = jnp.dot(perm, z[half - (i + 1) * blk:half - (i - 1) * blk], preferred_element_type=F32)
        o_ref[half + i * blk:half + (i + 1) * blk, :] = w.astype(BF16)


def _fourier(p, w_f, layer, *, batch, seq_len):
    ch, sh, cc, sc, perm, alt = _dft_constants(seq_len)
    half = seq_len // 2
    full = lambda t: pl.BlockSpec(t.shape, lambda b: (0,) * t.ndim)
    return pl.pallas_call(
        _fourier_kernel,
        grid=(batch,),
        in_specs=[pl.BlockSpec((seq_len, FOURIER_W), lambda b: (b, 0)),
                  _layer_spec(w_f, layer, lambda b: (0, 0, 0)),
                  full(cc), full(sc), full(ch), full(sh), full(perm), full(alt)],
        out_specs=pl.BlockSpec((seq_len, FOURIER_W), lambda b: (b, 0)),
        out_shape=jax.ShapeDtypeStruct((batch * seq_len, FOURIER_W), BF16),
        scratch_shapes=[pltpu.VMEM((half, FOURIER_W), BF16), pltpu.VMEM((half, FOURIER_W), BF16)],
        compiler_params=_params("arbitrary"),
        name="fourier",
    )(p, w_f, cc, sc, ch, sh, perm, alt)


CONV_PAD = 16
CONV_ROWS = 128
CONV_LEAD = CONV_PAD - CONV_K // 2


def _conv_fill(z_ref, zp_scr):
    seq_len = z_ref.shape[0]
    zp_scr[0:CONV_PAD, :] = jnp.zeros((CONV_PAD, CONV_W), F32)
    zp_scr[CONV_PAD + seq_len:2 * CONV_PAD + seq_len, :] = jnp.zeros((CONV_PAD, CONV_W), F32)
    zp_scr[CONV_PAD:CONV_PAD + seq_len, :] = z_ref[...].astype(F32)


def _conv_block(zp_scr, base, wdw_ref, bdw_ref, gln_ref, bln_ref, wpw_ref, bpw_ref):
    span = CONV_ROWS + 2 * CONV_PAD - SUBLANES
    cols = []
    for c in range(CONV_W // LANES):
        win = zp_scr[pl.ds(base, CONV_ROWS + 2 * CONV_PAD), c * LANES:(c + 1) * LANES]
        acc = None
        for shift in range(SUBLANES):
            shifted = win if shift == 0 else pltpu.roll(win, win.shape[0] - shift, 0)
            for a in range(span // SUBLANES):
                k = SUBLANES * a + shift - CONV_LEAD
                if 0 <= k < CONV_K and SUBLANES * a + CONV_ROWS <= span:
                    term = (shifted[SUBLANES * a:SUBLANES * a + CONV_ROWS]
                            * wdw_ref[k:k + 1, c * LANES:(c + 1) * LANES])
                    acc = term if acc is None else acc + term
        cols.append(acc)
    z = jnp.concatenate(cols, axis=1) + bdw_ref[...]
    zc = z - jnp.mean(z, axis=-1, keepdims=True)
    y = zc * lax.rsqrt(jnp.mean(zc * zc, axis=-1, keepdims=True) + EPS) * gln_ref[...] + bln_ref[...]
    s = (y * jax.nn.sigmoid(y)).astype(BF16)
    return (jnp.dot(s, wpw_ref[...], preferred_element_type=F32) + bpw_ref[...]).astype(BF16)


def _conv_kernel(z_ref, wdw_ref, bdw_ref, gln_ref, bln_ref, wpw_ref, bpw_ref, o_ref, zp_scr):
    _conv_fill(z_ref, zp_scr)

    def body(r, carry):
        base = pl.multiple_of(r * CONV_ROWS, CONV_ROWS)
        o_ref[pl.ds(base, CONV_ROWS), :] = _conv_block(zp_scr, base, wdw_ref, bdw_ref, gln_ref, bln_ref,
                                                       wpw_ref, bpw_ref)
        return carry

    lax.fori_loop(0, z_ref.shape[0] // CONV_ROWS, body, 0)


def _conv(p, w_dw, b_dw, g_ln, b_ln, w_pw, layer, b_pw, *, batch, seq_len):
    vec = pl.BlockSpec((1, CONV_W), lambda b: (0, 0))
    return pl.pallas_call(
        _conv_kernel,
        grid=(batch,),
        in_specs=[pl.BlockSpec((seq_len, CONV_W), lambda b: (b, GLU_SLOT)),
                  pl.BlockSpec(w_dw.shape, lambda b: (0, 0)),
                  vec, vec, vec,
                  _layer_spec(w_pw, layer, lambda b: (0, 0)),
                  vec],
        out_specs=pl.BlockSpec((seq_len, CONV_W), lambda b: (b, 0)),
        out_shape=jax.ShapeDtypeStruct((batch * seq_len, CONV_W), BF16),
        scratch_shapes=[pltpu.VMEM((seq_len + 2 * CONV_PAD, CONV_W), F32)],
        compiler_params=_params("arbitrary"),
        name="conv_module",
    )(p, w_dw, b_dw.reshape(1, -1), g_ln.reshape(1, -1), b_ln.reshape(1, -1), w_pw, b_pw.reshape(1, -1))


OUT_PROJ_GROUP = 256


def _out_proj_kernel(yf_ref, yc_ref, ao_ref, w_ref, x_ref, gate_ref, gpost_ref, o_ref, w_scr):
    @pl.when(pl.program_id(0) == 0)
    def _():
        w_scr[...] = w_ref[...].astype(BF16)

    rows = OUT_PROJ_GROUP
    groups = [slice(r * rows, (r + 1) * rows) for r in range(x_ref.shape[0] // rows)]
    ys = []
    for sl in groups:
        y = jnp.dot(yf_ref[sl, :], w_scr[0:FOURIER_W, :], preferred_element_type=F32)
        y = y + jnp.dot(yc_ref[sl, :], w_scr[FOURIER_W:FOURIER_W + CONV_W, :], preferred_element_type=F32)
        ys.append(y + jnp.dot(ao_ref[sl, :], w_scr[FOURIER_W + CONV_W:, :], preferred_element_type=F32))
    for sl, y in zip(groups, ys):
        o_ref[sl, :] = x_ref[sl, :] + gate_ref[...] * (_rms(y) * gpost_ref[...])


def _out_proj(yf, yc, ao, w, layer, x, modv, row0, g_post, *, group_rows, tm):
    m, d = x.shape
    bpg = group_rows // tm
    return pl.pallas_call(
        _out_proj_kernel,
        grid=(m // tm,),
        in_specs=[pl.BlockSpec((tm, FOURIER_W), lambda i: (i, 0)),
                  pl.BlockSpec((tm, CONV_W), lambda i: (i, 0)),
                  pl.BlockSpec((tm, ATT_W), lambda i: (i, 0)),
                  _layer_spec(w, layer, lambda i: (0, 0), pipeline_mode=pl.Buffered(1)),
                  pl.BlockSpec((tm, d), lambda i: (i, 0)),
                  _mod_spec(d, layer, row0, bpg, 2),
                  pl.BlockSpec((1, d), lambda i: (0, 0))],
        out_specs=pl.BlockSpec((tm, d), lambda i: (i, 0)),
        out_shape=jax.ShapeDtypeStruct((m, d), F32),
        scratch_shapes=[pltpu.VMEM(w.shape[1:], BF16)],
        compiler_params=_params("arbitrary"),
        name="out_proj",
    )(yf, yc, ao, w, x, modv, g_post.reshape(1, d))


def _mlp_kernel(x_ref, g_ref, sh_ref, sc_ref, gate_ref, gpost_ref, w1_ref, w2_ref, o_ref, h_scr):
    j = pl.program_id(1)
    last = pl.num_programs(1) - 1

    def ff_chunk(hb):
        h1 = jnp.maximum(jnp.dot(hb, w1_ref[...].astype(BF16), preferred_element_type=F32), 0.0)
        return jnp.dot((h1 * h1).astype(BF16), w2_ref[...].astype(BF16), preferred_element_type=F32)

    @pl.when(j == 0)
    def _():
        h = _rms(x_ref[...]) * g_ref[...]
        hb = (h * (1.0 + sc_ref[...]) + sh_ref[...]).astype(BF16)
        h_scr[...] = hb
        o_ref[...] = ff_chunk(hb)

    @pl.when(jnp.logical_and(j > 0, j < last))
    def _():
        o_ref[...] += ff_chunk(h_scr[...])

    @pl.when(j == last)
    def _():
        y = o_ref[...] + ff_chunk(h_scr[...])
        o_ref[...] = x_ref[...] + gate_ref[...] * (_rms(y) * gpost_ref[...])


def _mlp(x, g_pre, modv, row0, g_post, w1, w2, layer, *, group_rows, tm, tf):
    m, d = x.shape
    dff = w1.shape[-1]
    bpg = group_rows // tm
    mod_spec = lambda k: _mod_spec(d, layer, row0, bpg, k)
    return pl.pallas_call(
        _mlp_kernel,
        grid=(m // tm, dff // tf),
        in_specs=[pl.BlockSpec((tm, d), lambda i, j: (i, 0)),
                  pl.BlockSpec((1, d), lambda i, j: (0, 0)),
                  mod_spec(3), mod_spec(4), mod_spec(5),
                  pl.BlockSpec((1, d), lambda i, j: (0, 0)),
                  _layer_spec(w1, layer, lambda i, j: (0, j), block_tail=(d, tf)),
                  _layer_spec(w2, layer, lambda i, j: (j, 0), block_tail=(tf, d))],
        out_specs=pl.BlockSpec((tm, d), lambda i, j: (i, 0)),
        out_shape=jax.ShapeDtypeStruct((m, d), F32),
        scratch_shapes=[pltpu.VMEM((tm, d), BF16)],
        compiler_params=pltpu.CompilerParams(dimension_semantics=("arbitrary", "arbitrary"),
                                             vmem_limit_bytes=V7X_VMEM_LIMIT_MLP_BYTES),
        name="mlp",
    )(x, g_pre.reshape(1, d), modv, modv, modv, g_post.reshape(1, d), w1, w2)


def _rope_tables(n):
    rows = n // GRID_W
    row = np.repeat(np.arange(rows), GRID_W).astype(np.float64)
    col = np.tile(np.arange(GRID_W), rows).astype(np.float64)
    n_freq = DIFF_HEAD_DIM // 4
    inv = ROPE_BASE ** (-np.arange(n_freq, dtype=np.float64) / n_freq)
    ang = np.concatenate([row[:, None] * inv, col[:, None] * inv], axis=-1)
    cos, sin = np.cos(ang), np.sin(ang)
    return (jnp.asarray(np.concatenate([cos, cos, cos, cos], axis=-1), F32),
            jnp.asarray(np.concatenate([-sin, sin, -sin, sin], axis=-1), F32))


IN_PROJ_ROWS = 512
OUT_PROJ_ROWS = 512
MLP_ROWS = 1024
MLP_FF_CHUNK = 512
CTX_ATT_HEADS = 4


def kernel(x, c, ctx, c_ctx, w_ada, b_ada, g_pre_mix, g_post_mix, g_pre_mlp, g_post_mlp, w_in, w_out, w_fourier, w_dw, b_dw, g_conv_ln, b_conv_ln, w_conv_pw, b_conv_pw, lambda_q1, lambda_k1, lambda_q2, lambda_k2, g_subln, w_mlp_in, w_mlp_out):
    bsz, n, d = x.shape
    n_ctx = ctx.shape[1]
    depth = w_ada.shape[0]
    n_lat_rows, n_ctx_rows = bsz * n, bsz * n_ctx

    cvec = jnp.concatenate([c, c_ctx[None, :], jnp.zeros((SUBLANES - bsz - 1, d), F32)], axis=0)
    modv = _adaln(cvec, w_ada, b_ada).reshape(depth, SUBLANES, 6, 1, d)
    lamp_all = jnp.pad(jnp.stack([lambda_q1, lambda_k1, lambda_q2, lambda_k2], axis=1),
                       ((0, 0), (0, SUBLANES - 4), (0, LANES - DIFF_HEAD_DIM)))

    w_pw_b = w_conv_pw.astype(BF16)
    cos, sin = _rope_tables(n)

    xl = x.reshape(n_lat_rows, d)
    xc = ctx.reshape(n_ctx_rows, d)
    col = lambda chunks, j: _in_proj_out_slots(chunks)[j] * CHUNK // LANES
    q_col, k_col, v_col = (col(ALL_CHUNKS, j) for j in (Q_CHUNKS[0], K_CHUNKS[0], KV_CHUNKS[2]))

    for l in range(depth):
        last = l == depth - 1
        lam_init = 0.8 - 0.6 * math.exp(-0.3 * l)
        lamp = lamp_all[l]
        gsub = g_subln[l].reshape(1, LANES)

        p_lat = _in_proj(xl, g_pre_mix[l], modv, 0, w_in, l, cos, sin,
                         chunks=ALL_CHUNKS, group_rows=n, tm=IN_PROJ_ROWS)
        ctx_chunks = KV_CHUNKS if last else ALL_CHUNKS
        p_ctx = _in_proj(xc, g_pre_mix[l], modv, bsz, w_in, l, None, None,
                         chunks=ctx_chunks, group_rows=n_ctx_rows, tm=IN_PROJ_ROWS)
        kc_col, vc_col = col(ctx_chunks, K_CHUNKS[0]), col(ctx_chunks, KV_CHUNKS[2])

        conv_args = (w_dw[l], b_dw[l], g_conv_ln[l], b_conv_ln[l], w_pw_b, l, b_conv_pw[l])
        ao, yc = _attention(p_lat, q_col, [(p_ctx, kc_col, vc_col, n_ctx), (p_lat, k_col, v_col, n)],
                            lamp, gsub, batch=bsz, q_len=n, tq=n, heads=1, lam_init=lam_init,
                            conv=(p_lat,) + conv_args)
        yf = _fourier(p_lat, w_fourier, l, batch=bsz, seq_len=n)
        xl_mid = _out_proj(yf, yc, ao, w_out, l, xl, modv, 0, g_post_mix[l], group_rows=n, tm=OUT_PROJ_ROWS)

        if not last:
            aoc = _attention(p_ctx, q_col, [(p_ctx, kc_col, vc_col, n_ctx)], lamp, gsub,
                             batch=bsz, q_len=n_ctx, tq=n_ctx, heads=CTX_ATT_HEADS, lam_init=lam_init)
            yfc = _fourier(p_ctx, w_fourier, l, batch=bsz, seq_len=n_ctx)
            ycc = _conv(p_ctx, *conv_args, batch=bsz, seq_len=n_ctx)
            xc_mid = _out_proj(yfc, ycc, aoc, w_out, l, xc, modv, bsz, g_post_mix[l],
                               group_rows=n_ctx_rows, tm=OUT_PROJ_ROWS)
            xc = _mlp(xc_mid, g_pre_mlp[l], modv, bsz, g_post_mlp[l], w_mlp_in, w_mlp_out, l,
                      group_rows=n_ctx_rows, tm=MLP_ROWS, tf=MLP_FF_CHUNK)

        xl = _mlp(xl_mid, g_pre_mlp[l], modv, 0, g_post_mlp[l], w_mlp_in, w_mlp_out, l,
                  group_rows=n, tm=MLP_ROWS, tf=MLP_FF_CHUNK)

    return xl.reshape(bsz, n, d)
```

```python
import functools
import math

import numpy as np
import jax
import jax.numpy as jnp
from jax import lax
from jax.experimental import pallas as pl
from jax.experimental.pallas import tpu as pltpu

F32 = jnp.float32
BF16 = jnp.bfloat16

D_MODEL = 2048
GRID_W = 64
FOURIER_W = 512
FOURIER_GROUPS = 4
FOURIER_GW = 128
CONV_W = 512
CONV_K = 31
ATT_W = 1024
DIFF_HEAD_DIM = 64
DIFF_HEADS = 8
D_FF = 4 * D_MODEL
ROPE_BASE = 10000.0
EPS = 1e-6
IN_COLS = 4608

CHUNK = 512
Q_CHUNKS = (3, 4)
K_CHUNKS = (5, 6)
ALL_CHUNKS = tuple(range(IN_COLS // CHUNK))
KV_CHUNKS = (5, 6, 7, 8)

Q_SCALE = DIFF_HEAD_DIM ** -0.5 * math.log2(math.e)

LANES = 128
SUBLANES = 8
V7X_VMEM_LIMIT_BYTES = 56 * 1024 * 1024
V7X_VMEM_LIMIT_MLP_BYTES = 62 * 1024 * 1024


def _params(*semantics):
    return pltpu.CompilerParams(dimension_semantics=semantics,
                                vmem_limit_bytes=V7X_VMEM_LIMIT_BYTES)


def _rms(y):
    return y * lax.rsqrt(jnp.mean(y * y, axis=-1, keepdims=True) + EPS)


def _mod_spec(d, layer, row0, blocks_per_row, k):
    return pl.BlockSpec((None, None, None, 1, d), lambda i, *_: (layer, row0 + i // blocks_per_row, k, 0, 0))


def _layer_row(v, layer):
    depth, n = v.shape
    return pl.BlockSpec((None, 1, n), lambda *_: (layer, 0, 0)), v.reshape(depth, 1, n)


def _layer_spec(w, layer, index_map_tail, block_tail=None, **kw):
    block_tail = w.shape[1:] if block_tail is None else block_tail
    return pl.BlockSpec((None,) + tuple(block_tail), lambda *g: (layer,) + tuple(index_map_tail(*g)), **kw)


def _adaln_kernel(c_ref, w_ref, b_ref, o_ref):
    s = jax.nn.silu(c_ref[...]).astype(BF16)
    o_ref[...] = jnp.dot(s, w_ref[...].astype(BF16), preferred_element_type=F32) + b_ref[...]


def _adaln(cvec, w_ada, b_ada):
    depth, d, n6 = w_ada.shape
    tn = 1024
    return pl.pallas_call(
        _adaln_kernel,
        grid=(depth, n6 // tn),
        in_specs=[pl.BlockSpec((8, d), lambda l, j: (0, 0)),
                  pl.BlockSpec((None, d, tn), lambda l, j: (l, 0, j)),
                  pl.BlockSpec((None, 1, tn), lambda l, j: (l, 0, j))],
        out_specs=pl.BlockSpec((None, 8, tn), lambda l, j: (l, 0, j)),
        out_shape=jax.ShapeDtypeStruct((depth, 8, n6), F32),
        compiler_params=_params("arbitrary", "arbitrary"),
        name="adaln",
    )(cvec, w_ada, b_ada.reshape(depth, 1, n6))


CONV_A_CHUNK = 1
CONV_GATE_CHUNK = 2


def _in_proj_out_slots(chunks):
    slots, nxt = {}, 0
    for j in chunks:
        if j != CONV_A_CHUNK:
            slots[j] = nxt
            nxt += 1
    return slots


GLU_SLOT = _in_proj_out_slots(ALL_CHUNKS)[CONV_GATE_CHUNK]


def _in_proj_kernel(*refs, chunks, rope, layer):
    if rope:
        x_ref, g_ref, sh_ref, sc_ref, w_hbm, cos_ref, sin_ref, o_ref, w_scr, stage, sem = refs
    else:
        x_ref, g_ref, sh_ref, sc_ref, w_hbm, o_ref, w_scr, stage, sem = refs

    def chunk_copy(n):
        slot = n % 2
        return pltpu.make_async_copy(w_hbm.at[layer, :, pl.ds(chunks[n] * CHUNK, CHUNK)],
                                     stage.at[slot], sem.at[slot])

    def normed():
        h = _rms(x_ref[...]) * g_ref[...]
        return (h * (1.0 + sc_ref[...]) + sh_ref[...]).astype(BF16)

    def project_all(hb, fetch_chunk):
        slots = _in_proj_out_slots(chunks)
        conv_a = None
        for n, j in enumerate(chunks):
            if fetch_chunk is not None:
                fetch_chunk(n)
            acc = jnp.dot(hb, w_scr[:, n * CHUNK:(n + 1) * CHUNK], preferred_element_type=F32)
            if j == CONV_A_CHUNK:
                conv_a = acc
                continue
            col0 = slots[j] * CHUNK
            if j == CONV_GATE_CHUNK:
                o_ref[:, col0:col0 + CHUNK] = (conv_a * jax.nn.sigmoid(acc)).astype(BF16)
            elif j in Q_CHUNKS or j in K_CHUNKS:
                if rope:
                    cos = cos_ref[...]
                    sin = sin_ref[...]
                    lane = lax.broadcasted_iota(jnp.int32, cos.shape, 1)
                    first_half = (lane % DIFF_HEAD_DIM) < (DIFF_HEAD_DIM // 2)
                for s in range(CHUNK // LANES):
                    t = acc[:, s * LANES:(s + 1) * LANES]
                    if rope:
                        swapped = jnp.where(first_half,
                                            pltpu.roll(t, LANES - DIFF_HEAD_DIM // 2, 1),
                                            pltpu.roll(t, DIFF_HEAD_DIM // 2, 1))
                        t = t * cos + swapped * sin
                    if j in Q_CHUNKS:
                        t = t * Q_SCALE
                    o_ref[:, col0 + s * LANES:col0 + (s + 1) * LANES] = t.astype(BF16)
            else:
                o_ref[:, col0:col0 + CHUNK] = acc.astype(BF16)

    @pl.when(pl.program_id(0) == 0)
    def _():
        def fetch_chunk(n):
            if n + 1 < len(chunks):
                chunk_copy(n + 1).start()
            chunk_copy(n).wait()
            w_scr[:, n * CHUNK:(n + 1) * CHUNK] = stage[n % 2].astype(BF16)

        chunk_copy(0).start()
        project_all(normed(), fetch_chunk)

    @pl.when(pl.program_id(0) != 0)
    def _():
        project_all(normed(), None)


def _in_proj(x, g_pre, modv, row0, w, layer, cos, sin, *, chunks, group_rows, tm):
    m, d = x.shape
    bpg = group_rows // tm
    rope = cos is not None
    g_spec, g_arg = _layer_row(g_pre, layer)
    in_specs = [pl.BlockSpec((tm, d), lambda i: (i, 0)),
                g_spec,
                _mod_spec(d, layer, row0, bpg, 0),
                _mod_spec(d, layer, row0, bpg, 1),
                pl.BlockSpec(memory_space=pl.ANY)]
    args = [x, g_arg, modv, modv, w]
    if rope:
        in_specs += [pl.BlockSpec((tm, LANES), lambda i: (i % bpg, 0)),
                     pl.BlockSpec((tm, LANES), lambda i: (i % bpg, 0))]
        args += [cos, sin]
    ncol = len(_in_proj_out_slots(chunks)) * CHUNK
    return pl.pallas_call(
        functools.partial(_in_proj_kernel, chunks=chunks, rope=rope, layer=layer),
        grid=(m // tm,),
        in_specs=in_specs,
        out_specs=pl.BlockSpec((tm, ncol), lambda i: (i, 0)),
        out_shape=jax.ShapeDtypeStruct((m, ncol), BF16),
        scratch_shapes=[pltpu.VMEM((d, len(chunks) * CHUNK), BF16),
                        pltpu.VMEM((2, d, CHUNK), F32),
                        pltpu.SemaphoreType.DMA((2,))],
        compiler_params=_params("arbitrary"),
        name="in_proj",
    )(*args)


ATT_UNIT_ROWS = 512


def _attn_kernel(*refs, kv_lens, heads, lam_init, conv_rows):
    n_kv = len(kv_lens)
    q_ref = refs[0]
    kv_refs = refs[1:1 + 2 * n_kv]
    if conv_rows:
        (lamp_ref, gsub_ref, z_ref, wdw_ref, bdw_ref, gln_ref, bln_ref, wpw_ref, bpw_ref,
         o_ref, yc_ref, kt_scr, v_scr, zp_scr) = refs[1 + 2 * n_kv:]

        @pl.when(jnp.logical_and(pl.program_id(1) == 0, pl.program_id(2) == 0))
        def _():
            _conv_fill(z_ref, zp_scr)

        conv_blocks = list(range(conv_rows // CONV_ROWS))
        conv_base = (pl.program_id(1) * pl.num_programs(2) + pl.program_id(2)) * conv_rows
    else:
        lamp_ref, gsub_ref, o_ref, kt_scr, v_scr = refs[1 + 2 * n_kv:]
        conv_blocks = []
    head_cols = [slice(h * LANES, (h + 1) * LANES) for h in range(heads)]

    @pl.when(pl.program_id(2) == 0)
    def _():
        for h, cols in enumerate(head_cols):
            off = 0
            for i, n in enumerate(kv_lens):
                kt_scr[h, :, off:off + n] = kv_refs[2 * i][:, cols].T
                v_scr[h, off:off + n, 0:LANES] = kv_refs[2 * i + 1][:, cols]
                off += n
            v_scr[h, :, LANES:2 * LANES] = jnp.ones((off, LANES), BF16)

    lp = lamp_ref[...]
    lam = (jnp.exp(jnp.sum(lp[0:1] * lp[1:2], axis=-1, keepdims=True))
           - jnp.exp(jnp.sum(lp[2:3] * lp[3:4], axis=-1, keepdims=True)) + lam_init)
    rows = min(ATT_UNIT_ROWS, q_ref.shape[0])
    units = [(h, u) for h in range(heads) for u in range(q_ref.shape[0] // rows)]

    def scores(unit):
        h, u = unit
        q = q_ref[u * rows:(u + 1) * rows, head_cols[h]]
        lane = lax.broadcasted_iota(jnp.int32, q.shape, 1)
        zero = jnp.zeros_like(q)
        qq = jnp.concatenate([jnp.where(lane < DIFF_HEAD_DIM, q, zero),
                              jnp.where(lane < DIFF_HEAD_DIM, zero, q)], axis=0)
        return jnp.dot(qq, kt_scr[h], preferred_element_type=F32)

    s_next = scores(units[0])
    for n, (h, u) in enumerate(units):
        s = s_next
        if n + 1 < len(units):
            s_next = scores(units[n + 1])
        e = jnp.exp2(s - s.max(axis=-1, keepdims=True)).astype(BF16)
        acc1 = jnp.dot(e[:rows], v_scr[h], preferred_element_type=F32)
        acc2 = jnp.dot(e[rows:], v_scr[h], preferred_element_type=F32)
        o = (acc1[:, 0:LANES] / acc1[:, LANES:2 * LANES]
             - lam * (acc2[:, 0:LANES] / acc2[:, LANES:2 * LANES]))
        o_ref[u * rows:(u + 1) * rows, head_cols[h]] = (
            _rms(o) * gsub_ref[...] * (1.0 - lam_init)).astype(BF16)
        for c in [c for c in conv_blocks if c * len(units) // len(conv_blocks) == n]:
            base = pl.multiple_of(conv_base + c * CONV_ROWS, CONV_ROWS)
            yc_ref[c * CONV_ROWS:(c + 1) * CONV_ROWS, :] = _conv_block(
                zp_scr, base, wdw_ref, bdw_ref, gln_ref, bln_ref, wpw_ref, bpw_ref)


def _conv_param_specs(w_dw, b_dw, g_ln, b_ln, w_pw, b_pw, layer):
    rows = [_layer_row(v, layer) for v in (b_dw, g_ln, b_ln, b_pw)]
    zero2 = lambda *_: (0, 0)
    specs = [_layer_spec(w_dw, layer, zero2), rows[0][0], rows[1][0], rows[2][0],
             _layer_spec(w_pw, layer, zero2), rows[3][0]]
    return specs, [w_dw, rows[0][1], rows[1][1], rows[2][1], w_pw, rows[3][1]]


def _attention(q_arr, q_col, kvs, lamp, gsub, layer, *, batch, q_len, tq, heads, lam_init, conv=None):
    nqb = q_len // tq
    width = heads * LANES
    n_head_steps = DIFF_HEADS // heads
    assert all(c % heads == 0 for c in [q_col] + [kv[1] for kv in kvs] + [kv[2] for kv in kvs])
    in_specs = [pl.BlockSpec((tq, width), lambda b, h, i: (b * nqb + i, q_col // heads + h))]
    args = [q_arr]
    for arr, k_col, v_col, kv_len in kvs:
        in_specs.append(pl.BlockSpec((kv_len, width), lambda b, h, i, c=k_col // heads: (b, c + h)))
        in_specs.append(pl.BlockSpec((kv_len, width), lambda b, h, i, c=v_col // heads: (b, c + h)))
        args += [arr, arr]
    gsub_spec, gsub_arg = _layer_row(gsub, layer)
    in_specs += [_layer_spec(lamp, layer, lambda *_: (0, 0)), gsub_spec]
    args += [lamp, gsub_arg]
    kv_lens = tuple(kv[3] for kv in kvs)
    out_specs = pl.BlockSpec((tq, width), lambda b, h, i: (b * nqb + i, h))
    out_shape = jax.ShapeDtypeStruct((batch * q_len, ATT_W), BF16)
    scratch = [pltpu.VMEM((heads, LANES, sum(kv_lens)), BF16),
               pltpu.VMEM((heads, sum(kv_lens), 2 * LANES), BF16)]
    conv_rows = 0
    if conv is not None:
        p, *conv_params = conv
        conv_rows = q_len // (n_head_steps * nqb)
        assert conv_rows % CONV_ROWS == 0
        conv_specs, conv_operands = _conv_param_specs(*conv_params, layer)
        in_specs += [pl.BlockSpec((q_len, CONV_W), lambda b, h, i: (b, GLU_SLOT))] + conv_specs
        args += [p] + conv_operands
        out_specs = [out_specs,
                     pl.BlockSpec((conv_rows, CONV_W), lambda b, h, i: ((b * n_head_steps + h) * nqb + i, 0))]
        out_shape = [out_shape, jax.ShapeDtypeStruct((batch * q_len, CONV_W), BF16)]
        scratch.append(pltpu.VMEM((q_len + 2 * CONV_PAD, CONV_W), F32))
    return pl.pallas_call(
        functools.partial(_attn_kernel, kv_lens=kv_lens, heads=heads, lam_init=lam_init, conv_rows=conv_rows),
        grid=(batch, n_head_steps, nqb),
        in_specs=in_specs,
        out_specs=out_specs,
        out_shape=out_shape,
        scratch_shapes=scratch,
        compiler_params=_params("arbitrary", "arbitrary", "arbitrary"),
        name="diff_attn",
    )(*args)


FOURIER_BLK = 128


def _dft_constants(seq_len):
    half = seq_len // 2
    k = np.arange(half, dtype=np.int64)
    ang = 2.0 * np.pi * ((k[:, None] * k[None, :]) % seq_len).astype(np.float64) / seq_len
    ch = np.cos(ang).astype(np.float32)
    sh = np.sin(ang).astype(np.float32)
    c = np.arange(FOURIER_GW, dtype=np.int64)
    angc = 2.0 * np.pi * ((c[:, None] * c[None, :]) % FOURIER_GW).astype(np.float64) / FOURIER_GW
    norm = 1.0 / math.sqrt(seq_len * FOURIER_GW)
    cc = (np.cos(angc) * norm).astype(np.float32)
    sc = (np.sin(angc) * norm).astype(np.float32)
    perm = np.zeros((FOURIER_BLK, 2 * FOURIER_BLK), np.float32)
    r = np.arange(1, FOURIER_BLK)
    perm[r, FOURIER_BLK - r] = 1.0
    perm[0, FOURIER_BLK] = 1.0
    alt = np.zeros((SUBLANES, half), np.float32)
    alt[0] = 1.0 - 2.0 * (k % 2)
    return tuple(jnp.asarray(t).astype(BF16) for t in (ch, sh, cc, sc, perm, alt))


def _fourier_kernel(u_ref, wf_ref, cc_ref, sc_ref, ch_ref, sh_ref, perm_ref, alt_ref, o_ref, r1_scr, r2_scr):
    seq_len = u_ref.shape[0]
    half = seq_len // 2
    blk = FOURIER_BLK
    nb = half // blk
    groups = [slice(g * FOURIER_GW, (g + 1) * FOURIER_GW) for g in range(FOURIER_GROUPS)]
    perm = perm_ref[...]
    rev = perm[:, 0:blk]
    mix_a, mix_b = [], []
    for g in groups:
        wf = wf_ref[g.start // FOURIER_GW].astype(BF16)
        mix_a.append(jnp.dot(cc_ref[...], wf, preferred_element_type=F32).astype(BF16))
        mix_b.append(jnp.dot(sc_ref[...], wf, preferred_element_type=F32).astype(BF16))

    for i in range(nb):
        if i == 0:
            u_rev = jnp.dot(rev, u_ref[seq_len - blk:seq_len, :], preferred_element_type=F32)
        else:
            u_rev = jnp.dot(perm, u_ref[seq_len - (i + 1) * blk:seq_len - (i - 1) * blk, :],
                            preferred_element_type=F32)
        u_blk = u_ref[i * blk:(i + 1) * blk, :].astype(F32)
        ue = (u_blk + u_rev).astype(BF16)
        uo = (u_blk - u_rev).astype(BF16)
        for g, a, b in zip(groups, mix_a, mix_b):
            r1_scr[i * blk:(i + 1) * blk, g] = jnp.dot(ue[:, g], a, preferred_element_type=F32).astype(BF16)
            r2_scr[i * blk:(i + 1) * blk, g] = jnp.dot(uo[:, g], b, preferred_element_type=F32).astype(BF16)

    u_mid = u_ref[half:half + SUBLANES, :]
    t_mid = jnp.concatenate([jnp.dot(u_mid[:, g], a, preferred_element_type=F32)
                             for g, a in zip(groups, mix_a)], axis=1)[0:1]
    k = lax.broadcasted_iota(jnp.int32, (half, 1), 0)
    sign = jnp.where(k % 2 == 0, 1.0, -1.0).astype(F32)
    p = jnp.dot(ch_ref[...], r1_scr[...], preferred_element_type=F32) + sign * t_mid
    q = jnp.dot(sh_ref[...], r2_scr[...], preferred_element_type=F32)
    o_ref[0:half, :] = (p - q).astype(BF16)
    z = (p + q).astype(BF16)
    y_mid = jnp.dot(alt_ref[...], r1_scr[...], preferred_element_type=F32)[0:1] + t_mid
    row = lax.broadcasted_iota(jnp.int32, (blk, 1), 0)
    for i in range(nb):
        if i == 0:
            w = jnp.dot(rev, z[half - blk:half], preferred_element_type=F32)
            w = jnp.where(row == 0, y_mid, w)
        else:
            w = jnp.dot(perm, z[half - (i + 1) * blk:half - (i - 1) * blk], preferred_element_type=F32)
        o_ref[half + i * blk:half + (i + 1) * blk, :] = w.astype(BF16)


def _fourier(p, w_f, layer, *, batch, seq_len):
    ch, sh, cc, sc, perm, alt = _dft_constants(seq_len)
    half = seq_len // 2
    full = lambda t: pl.BlockSpec(t.shape, lambda b: (0,) * t.ndim)
    return pl.pallas_call(
        _fourier_kernel,
        grid=(batch,),
        in_specs=[pl.BlockSpec((seq_len, FOURIER_W), lambda b: (b, 0)),
                  _layer_spec(w_f, layer, lambda b: (0, 0, 0)),
                  full(cc), full(sc), full(ch), full(sh), full(perm), full(alt)],
        out_specs=pl.BlockSpec((seq_len, FOURIER_W), lambda b: (b, 0)),
        out_shape=jax.ShapeDtypeStruct((batch * seq_len, FOURIER_W), BF16),
        scratch_shapes=[pltpu.VMEM((half, FOURIER_W), BF16), pltpu.VMEM((half, FOURIER_W), BF16)],
        compiler_params=_params("arbitrary"),
        name="fourier",
    )(p, w_f, cc, sc, ch, sh, perm, alt)


CONV_PAD = 16
CONV_ROWS = 128
CONV_LEAD = CONV_PAD - CONV_K // 2


def _conv_fill(z_ref, zp_scr):
    seq_len = z_ref.shape[0]
    zp_scr[0:CONV_PAD, :] = jnp.zeros((CONV_PAD, CONV_W), F32)
    zp_scr[CONV_PAD + seq_len:2 * CONV_PAD + seq_len, :] = jnp.zeros((CONV_PAD, CONV_W), F32)
    zp_scr[CONV_PAD:CONV_PAD + seq_len, :] = z_ref[...].astype(F32)


def _conv_block(zp_scr, base, wdw_ref, bdw_ref, gln_ref, bln_ref, wpw_ref, bpw_ref):
    span = CONV_ROWS + 2 * CONV_PAD - SUBLANES
    cols = []
    for c in range(CONV_W // LANES):
        win = zp_scr[pl.ds(base, CONV_ROWS + 2 * CONV_PAD), c * LANES:(c + 1) * LANES]
        acc = None
        for shift in range(SUBLANES):
            shifted = win if shift == 0 else pltpu.roll(win, win.shape[0] - shift, 0)
            for a in range(span // SUBLANES):
                k = SUBLANES * a + shift - CONV_LEAD
                if 0 <= k < CONV_K and SUBLANES * a + CONV_ROWS <= span:
                    term = (shifted[SUBLANES * a:SUBLANES * a + CONV_ROWS]
                            * wdw_ref[k:k + 1, c * LANES:(c + 1) * LANES])
                    acc = term if acc is None else acc + term
        cols.append(acc)
    z = jnp.concatenate(cols, axis=1) + bdw_ref[...]
    zc = z - jnp.mean(z, axis=-1, keepdims=True)
    y = zc * lax.rsqrt(jnp.mean(zc * zc, axis=-1, keepdims=True) + EPS) * gln_ref[...] + bln_ref[...]
    s = (y * jax.nn.sigmoid(y)).astype(BF16)
    return (jnp.dot(s, wpw_ref[...], preferred_element_type=F32) + bpw_ref[...]).astype(BF16)


def _conv_kernel(z_ref, wdw_ref, bdw_ref, gln_ref, bln_ref, wpw_ref, bpw_ref, o_ref, zp_scr):
    _conv_fill(z_ref, zp_scr)

    def body(r, carry):
        base = pl.multiple_of(r * CONV_ROWS, CONV_ROWS)
        o_ref[pl.ds(base, CONV_ROWS), :] = _conv_block(zp_scr, base, wdw_ref, bdw_ref, gln_ref, bln_ref,
                                                       wpw_ref, bpw_ref)
        return carry

    lax.fori_loop(0, z_ref.shape[0] // CONV_ROWS, body, 0)


def _conv(p, w_dw, b_dw, g_ln, b_ln, w_pw, b_pw, layer, *, batch, seq_len):
    conv_specs, conv_operands = _conv_param_specs(w_dw, b_dw, g_ln, b_ln, w_pw, b_pw, layer)
    return pl.pallas_call(
        _conv_kernel,
        grid=(batch,),
        in_specs=[pl.BlockSpec((seq_len, CONV_W), lambda b: (b, GLU_SLOT))] + conv_specs,
        out_specs=pl.BlockSpec((seq_len, CONV_W), lambda b: (b, 0)),
        out_shape=jax.ShapeDtypeStruct((batch * seq_len, CONV_W), BF16),
        scratch_shapes=[pltpu.VMEM((seq_len + 2 * CONV_PAD, CONV_W), F32)],
        compiler_params=_params("arbitrary"),
        name="conv_module",
    )(p, *conv_operands)


OUT_PROJ_GROUP = 256


def _out_proj_kernel(yf_ref, yc_ref, ao_ref, w_ref, x_ref, gate_ref, gpost_ref, o_ref, w_scr):
    @pl.when(pl.program_id(0) == 0)
    def _():
        w_scr[...] = w_ref[...].astype(BF16)

    rows = OUT_PROJ_GROUP
    groups = [slice(r * rows, (r + 1) * rows) for r in range(x_ref.shape[0] // rows)]
    ys = []
    for sl in groups:
        y = jnp.dot(yf_ref[sl, :], w_scr[0:FOURIER_W, :], preferred_element_type=F32)
        y = y + jnp.dot(yc_ref[sl, :], w_scr[FOURIER_W:FOURIER_W + CONV_W, :], preferred_element_type=F32)
        ys.append(y + jnp.dot(ao_ref[sl, :], w_scr[FOURIER_W + CONV_W:, :], preferred_element_type=F32))
    for sl, y in zip(groups, ys):
        o_ref[sl, :] = x_ref[sl, :] + gate_ref[...] * (_rms(y) * gpost_ref[...])


def _out_proj(yf, yc, ao, w, layer, x, modv, row0, g_post, *, group_rows, tm):
    m, d = x.shape
    bpg = group_rows // tm
    g_spec, g_arg = _layer_row(g_post, layer)
    return pl.pallas_call(
        _out_proj_kernel,
        grid=(m // tm,),
        in_specs=[pl.BlockSpec((tm, FOURIER_W), lambda i: (i, 0)),
                  pl.BlockSpec((tm, CONV_W), lambda i: (i, 0)),
                  pl.BlockSpec((tm, ATT_W), lambda i: (i, 0)),
                  _layer_spec(w, layer, lambda i: (0, 0), pipeline_mode=pl.Buffered(1)),
                  pl.BlockSpec((tm, d), lambda i: (i, 0)),
                  _mod_spec(d, layer, row0, bpg, 2),
                  g_spec],
        out_specs=pl.BlockSpec((tm, d), lambda i: (i, 0)),
        out_shape=jax.ShapeDtypeStruct((m, d), F32),
        scratch_shapes=[pltpu.VMEM(w.shape[1:], BF16)],
        compiler_params=_params("arbitrary"),
        name="out_proj",
    )(yf, yc, ao, w, x, modv, g_arg)


def _mlp_kernel(x_ref, g_ref, sh_ref, sc_ref, gate_ref, gpost_ref, w1_ref, w2_ref, o_ref, h_scr):
    j = pl.program_id(1)
    last = pl.num_programs(1) - 1

    def ff_chunk(hb):
        h1 = jnp.maximum(jnp.dot(hb, w1_ref[...].astype(BF16), preferred_element_type=F32), 0.0)
        return jnp.dot((h1 * h1).astype(BF16), w2_ref[...].astype(BF16), preferred_element_type=F32)

    @pl.when(j == 0)
    def _():
        h = _rms(x_ref[...]) * g_ref[...]
        hb = (h * (1.0 + sc_ref[...]) + sh_ref[...]).astype(BF16)
        h_scr[...] = hb
        o_ref[...] = ff_chunk(hb)

    @pl.when(jnp.logical_and(j > 0, j < last))
    def _():
        o_ref[...] += ff_chunk(h_scr[...])

    @pl.when(j == last)
    def _():
        y = o_ref[...] + ff_chunk(h_scr[...])
        o_ref[...] = x_ref[...] + gate_ref[...] * (_rms(y) * gpost_ref[...])


def _mlp(x, g_pre, modv, row0, g_post, w1, w2, layer, *, group_rows, tm, tf):
    m, d = x.shape
    dff = w1.shape[-1]
    bpg = group_rows // tm
    mod_spec = lambda k: _mod_spec(d, layer, row0, bpg, k)
    (g_pre_spec, g_pre_arg), (g_post_spec, g_post_arg) = _layer_row(g_pre, layer), _layer_row(g_post, layer)
    return pl.pallas_call(
        _mlp_kernel,
        grid=(m // tm, dff // tf),
        in_specs=[pl.BlockSpec((tm, d), lambda i, j: (i, 0)),
                  g_pre_spec,
                  mod_spec(3), mod_spec(4), mod_spec(5),
                  g_post_spec,
                  _layer_spec(w1, layer, lambda i, j: (0, j), block_tail=(d, tf)),
                  _layer_spec(w2, layer, lambda i, j: (j, 0), block_tail=(tf, d))],
        out_specs=pl.BlockSpec((tm, d), lambda i, j: (i, 0)),
        out_shape=jax.ShapeDtypeStruct((m, d), F32),
        scratch_shapes=[pltpu.VMEM((tm, d), BF16)],
        compiler_params=pltpu.CompilerParams(dimension_semantics=("arbitrary", "arbitrary"),
                                             vmem_limit_bytes=V7X_VMEM_LIMIT_MLP_BYTES),
        name="mlp",
    )(x, g_pre_arg, modv, modv, modv, g_post_arg, w1, w2)


def _rope_tables(n):
    rows = n // GRID_W
    row = np.repeat(np.arange(rows), GRID_W).astype(np.float64)
    col = np.tile(np.arange(GRID_W), rows).astype(np.float64)
    n_freq = DIFF_HEAD_DIM // 4
    inv = ROPE_BASE ** (-np.arange(n_freq, dtype=np.float64) / n_freq)
    ang = np.concatenate([row[:, None] * inv, col[:, None] * inv], axis=-1)
    cos, sin = np.cos(ang), np.sin(ang)
    return (jnp.asarray(np.concatenate([cos, cos, cos, cos], axis=-1), F32),
            jnp.asarray(np.concatenate([-sin, sin, -sin, sin], axis=-1), F32))


IN_PROJ_ROWS = 512
OUT_PROJ_ROWS = 512
MLP_ROWS = 1024
MLP_FF_CHUNK = 512
CTX_ATT_HEADS = 4


def kernel(x, c, ctx, c_ctx, w_ada, b_ada, g_pre_mix, g_post_mix, g_pre_mlp, g_post_mlp, w_in, w_out, w_fourier, w_dw, b_dw, g_conv_ln, b_conv_ln, w_conv_pw, b_conv_pw, lambda_q1, lambda_k1, lambda_q2, lambda_k2, g_subln, w_mlp_in, w_mlp_out):
    bsz, n, d = x.shape
    n_ctx = ctx.shape[1]
    depth = w_ada.shape[0]
    n_lat_rows, n_ctx_rows = bsz * n, bsz * n_ctx

    cvec = jnp.concatenate([c, c_ctx[None, :], jnp.zeros((SUBLANES - bsz - 1, d), F32)], axis=0)
    modv = _adaln(cvec, w_ada, b_ada).reshape(depth, SUBLANES, 6, 1, d)
    lamp_all = jnp.pad(jnp.stack([lambda_q1, lambda_k1, lambda_q2, lambda_k2], axis=1),
                       ((0, 0), (0, SUBLANES - 4), (0, LANES - DIFF_HEAD_DIM)))

    w_pw_b = w_conv_pw.astype(BF16)
    cos, sin = _rope_tables(n)

    xl = x.reshape(n_lat_rows, d)
    xc = ctx.reshape(n_ctx_rows, d)
    col = lambda chunks, j: _in_proj_out_slots(chunks)[j] * CHUNK // LANES
    q_col, k_col, v_col = (col(ALL_CHUNKS, j) for j in (Q_CHUNKS[0], K_CHUNKS[0], KV_CHUNKS[2]))

    conv_params = (w_dw, b_dw, g_conv_ln, b_conv_ln, w_pw_b, b_conv_pw)
    for l in range(depth):
        last = l == depth - 1
        lam_init = 0.8 - 0.6 * math.exp(-0.3 * l)

        p_lat = _in_proj(xl, g_pre_mix, modv, 0, w_in, l, cos, sin,
                         chunks=ALL_CHUNKS, group_rows=n, tm=IN_PROJ_ROWS)
        ctx_chunks = KV_CHUNKS if last else ALL_CHUNKS
        p_ctx = _in_proj(xc, g_pre_mix, modv, bsz, w_in, l, None, None,
                         chunks=ctx_chunks, group_rows=n_ctx_rows, tm=IN_PROJ_ROWS)
        kc_col, vc_col = col(ctx_chunks, K_CHUNKS[0]), col(ctx_chunks, KV_CHUNKS[2])

        ao, yc = _attention(p_lat, q_col, [(p_ctx, kc_col, vc_col, n_ctx), (p_lat, k_col, v_col, n)],
                            lamp_all, g_subln, l, batch=bsz, q_len=n, tq=n, heads=1, lam_init=lam_init,
                            conv=(p_lat,) + conv_params)
        yf = _fourier(p_lat, w_fourier, l, batch=bsz, seq_len=n)
        xl_mid = _out_proj(yf, yc, ao, w_out, l, xl, modv, 0, g_post_mix, group_rows=n, tm=OUT_PROJ_ROWS)

        if not last:
            aoc = _attention(p_ctx, q_col, [(p_ctx, kc_col, vc_col, n_ctx)], lamp_all, g_subln, l,
                             batch=bsz, q_len=n_ctx, tq=n_ctx, heads=CTX_ATT_HEADS, lam_init=lam_init)
            yfc = _fourier(p_ctx, w_fourier, l, batch=bsz, seq_len=n_ctx)
            ycc = _conv(p_ctx, *conv_params, l, batch=bsz, seq_len=n_ctx)
            xc_mid = _out_proj(yfc, ycc, aoc, w_out, l, xc, modv, bsz, g_post_mix,
                               group_rows=n_ctx_rows, tm=OUT_PROJ_ROWS)
            xc = _mlp(xc_mid, g_pre_mlp, modv, bsz, g_post_mlp, w_mlp_in, w_mlp_out, l,
                      group_rows=n_ctx_rows, tm=MLP_ROWS, tf=MLP_FF_CHUNK)

        xl = _mlp(xl_mid, g_pre_mlp, modv, 0, g_post_mlp, w_mlp_in, w_mlp_out, l,
                  group_rows=n, tm=MLP_ROWS, tf=MLP_FF_CHUNK)

    return xl.reshape(bsz, n, d)
```
